```python
import jax, jax.numpy as jnp
from jax import lax
import numpy as np

D_MODEL = 1024
BATCH = 8
SEQ = 4096
DEPTH = 2

CHUNK = 64
MIX_W = D_MODEL
N_BRANCH = 3
RWKV_HEAD = 64
RWKV_HEADS = MIX_W // RWKV_HEAD
R_W = 64
R_A = 64
R_V = 32
SGU_BLOCK = 128
SGU_GROUPS = 8
SGU_GW = MIX_W // SGU_GROUPS
HGRN_HEAD = 128
HGRN_HEADS = MIX_W // HGRN_HEAD
EPS = 1e-6
GN_EPS = 64e-5
SPLIT_A = (MIX_W, MIX_W, MIX_W, R_W, R_A)
N_SHIFT = sum(SPLIT_A)
SPLIT_REST = (MIX_W,) * 8 + (N_BRANCH * D_MODEL,)
N_IN = N_SHIFT + sum(SPLIT_REST)

kernel_name = "hybrid_rwkv7_gmlp_hgrn2_encoder"


def _split(h, sizes):
    return jnp.split(h, np.cumsum(sizes)[:-1].tolist(), axis=-1)


def _rmsnorm(x, g):
    xf = x.astype(jnp.float32)
    y = xf * lax.rsqrt(jnp.mean(xf * xf, axis=-1, keepdims=True) + EPS)
    return (y * g.astype(jnp.float32)).astype(x.dtype)


def _token_shift(y):
    return jnp.pad(y[:, :-1], ((0, 0), (1, 0), (0, 0)))


def _rwkv7_scan(r, w, k, v, a, b):
    Bn, S, H, N = r.shape

    def step(state, inp):
        r_t, w_t, k_t, v_t, a_t, b_t = inp
        sa = jnp.einsum('bhij,bhj->bhi', state, a_t)
        state = (state * w_t[:, :, None, :]
                 + sa[..., None] * b_t[:, :, None, :]
                 + v_t[..., None] * k_t[:, :, None, :])
        y_t = jnp.einsum('bhij,bhj->bhi', state, r_t)
        return state, y_t

    s0 = jnp.zeros((Bn, H, N, N), jnp.float32)
    xs = tuple(jnp.moveaxis(z, 1, 0) for z in (r, w, k, v, a, b))
    _, y = lax.scan(step, s0, xs)
    return jnp.moveaxis(y, 0, 1)


def _rwkv7_branch(hA, zA, xn, v_first, mu, w0, w_w2, a0, a_w2, vres,
                  k_k, k_a, r_k, lnx_g, lnx_b):
    Bn, S, _ = hA.shape
    f32 = jnp.float32
    hA = hA + mu * (_token_shift(hA) - hA)
    r, k, v, wlat, alat = _split(hA, SPLIT_A)
    w_log = -jax.nn.softplus(-(w0 + jnp.tanh(wlat) @ w_w2)) - 0.5
    decay = jnp.exp(-jnp.exp(w_log.astype(f32)))
    a = jax.nn.sigmoid(a0 + alat @ a_w2)
    if vres is None:
        v_first = v
    else:
        v0, v_w1, v_w2 = vres
        v = v + (v_first - v) * jax.nn.sigmoid(v0 + (xn @ v_w1) @ v_w2)
    heads = lambda z: z.reshape(Bn, S, RWKV_HEADS, RWKV_HEAD).astype(f32)
    kk = heads(k * k_k)
    kk = kk / jnp.maximum(jnp.sqrt(jnp.sum(kk * kk, axis=-1, keepdims=True)), 1e-12)
    k = k * (1.0 + (a - 1.0) * k_a)
    rh, kh, vh = heads(r), heads(k), heads(v)
    y = _rwkv7_scan(rh, heads(decay), kh, vh, -kk, kk * heads(a))
    mean = jnp.mean(y, axis=-1, keepdims=True)
    var = jnp.var(y, axis=-1, keepdims=True)
    y = ((y - mean) * lax.rsqrt(var + GN_EPS)).reshape(Bn, S, MIX_W) * lnx_g + lnx_b
    r_k_h = r_k.reshape(RWKV_HEADS, RWKV_HEAD).astype(f32)
    bonus = jnp.sum(rh * kh * r_k_h, axis=-1, keepdims=True) * vh
    o = (y + bonus.reshape(Bn, S, MIX_W)).astype(zA.dtype) * jax.nn.silu(zA)
    return o, v_first


def _sgu_branch(uB, vB, zB, ln_g, ln_b, w_s, b_s):
    Bn, S, _ = vB.shape
    vf = vB.astype(jnp.float32)
    mean = jnp.mean(vf, axis=-1, keepdims=True)
    var = jnp.var(vf, axis=-1, keepdims=True)
    vn = ((vf - mean) * lax.rsqrt(var + EPS) * ln_g + ln_b).astype(vB.dtype)
    pos = jnp.arange(SGU_BLOCK)
    mask = (pos[:, None] // CHUNK) >= (pos[None, :] // CHUNK)
    ws = jnp.where(mask[None], w_s, 0.0)
    vb = vn.reshape(Bn, S // SGU_BLOCK, SGU_BLOCK, SGU_GROUPS, SGU_GW)
    mixed = jnp.einsum('gts,bnsgc->bntgc', ws, vb) + b_s.T[None, None, :, :, None]
    return uB * mixed.reshape(Bn, S, MIX_W) * jax.nn.silu(zB)


def _hgrn2_branch(qC, fC, iC, zC, lb, g_norm):
    Bn, S, _ = qC.shape
    f32 = jnp.float32
    NC = S // CHUNK
    q = jax.nn.silu(qC.astype(f32))
    fg = lb + (1.0 - lb) * jax.nn.sigmoid(fC.astype(f32))
    log_f = jnp.log(fg)
    k = 1.0 - fg
    i = iC.astype(f32)

    def chunks(z):
        return z.reshape(Bn, NC, CHUNK, HGRN_HEADS, HGRN_HEAD).transpose(1, 0, 3, 2, 4)

    q_c, k_c, i_c = chunks(q), chunks(k), chunks(i)
    b_c = jnp.cumsum(chunks(log_f), axis=3)
    causal = jnp.tril(jnp.ones((CHUNK, CHUNK), bool))

    def step(state, inp):
        q_t, k_t, i_t, b_t = inp
        rel = b_t[:, :, :, None, :] - b_t[:, :, None, :, :]
        dec = jnp.exp(jnp.where(causal[:, :, None], rel, -jnp.inf))
        att = jnp.einsum('bhtk,bhsk,bhtsk->bhts', q_t, k_t, dec)
        o = (jnp.einsum('bhts,bhsv->bhtv', att, i_t)
             + jnp.einsum('bhtk,bhkv->bhtv', q_t * jnp.exp(b_t), state))
        b_last = b_t[:, :, -1:, :]
        state = (jnp.exp(b_last[:, :, 0, :, None]) * state
                 + jnp.einsum('bhsk,bhsv->bhkv', k_t * jnp.exp(b_last - b_t), i_t))
        return state, o

    s0 = jnp.zeros((Bn, HGRN_HEADS, HGRN_HEAD, HGRN_HEAD), f32)
    _, o = lax.scan(step, s0, (q_c, k_c, i_c, b_c))
    o = o.transpose(1, 0, 3, 2, 4).reshape(Bn, S, HGRN_HEADS, HGRN_HEAD)
    o = o * lax.rsqrt(jnp.mean(o * o, axis=-1, keepdims=True) + EPS)
    o = o.reshape(Bn, S, MIX_W) * g_norm
    return o.astype(zC.dtype) * jax.nn.silu(zC)


def setup_inputs(seed: int = 0) -> dict:
    key = jax.random.key(seed)
    ks = jax.random.split(key, 26)
    f32 = jnp.float32
    nrm = lambda k, shape, s: s * jax.random.normal(k, shape, f32)
    L = DEPTH
    LV = max(DEPTH - 1, 0)
    return {
        "x": nrm(ks[0], (BATCH, SEQ, D_MODEL), 1.0),
        "norm_g": 1.0 + nrm(ks[1], (L, D_MODEL), 0.02),
        "w_in": nrm(ks[2], (L, D_MODEL, N_IN), D_MODEL ** -0.5),
        "b_merge": nrm(ks[3], (L, N_BRANCH, D_MODEL), 0.02),
        "mu_shift": jax.random.uniform(ks[4], (L, N_SHIFT), f32),
        "w0": -1.0 + nrm(ks[5], (L, MIX_W), 0.5),
        "w_w2": nrm(ks[6], (L, R_W, MIX_W), 0.1),
        "a0": nrm(ks[7], (L, MIX_W), 0.1),
        "a_w2": nrm(ks[8], (L, R_A, MIX_W), 0.1),
        "v0": 1.0 + nrm(ks[9], (LV, MIX_W), 0.1),
        "v_w1": nrm(ks[10], (LV, D_MODEL, R_V), D_MODEL ** -0.5),
        "v_w2": nrm(ks[11], (LV, R_V, MIX_W), 0.1),
        "k_k": 0.85 + nrm(ks[12], (L, MIX_W), 0.02),
        "k_a": 1.0 + nrm(ks[13], (L, MIX_W), 0.02),
        "r_k": nrm(ks[14], (L, MIX_W), 0.1),
        "lnx_g": 1.0 + nrm(ks[15], (L, MIX_W), 0.02),
        "lnx_b": nrm(ks[16], (L, MIX_W), 0.02),
        "sgu_ln_g": 1.0 + nrm(ks[17], (L, MIX_W), 0.02),
        "sgu_ln_b": nrm(ks[18], (L, MIX_W), 0.02),
        "sgu_w": nrm(ks[19], (L, SGU_GROUPS, SGU_BLOCK, SGU_BLOCK), 0.5 * SGU_BLOCK ** -0.5),
        "sgu_b": 1.0 + nrm(ks[20], (L, SGU_GROUPS, SGU_BLOCK), 0.02),
        "lb_logits": nrm(ks[21], (L, MIX_W), 0.1),
        "hgrn_g": 1.0 + nrm(ks[22], (L, MIX_W), 0.02),
        "w_branch": nrm(ks[23], (L, N_BRANCH, MIX_W, D_MODEL), MIX_W ** -0.5),
        "w_out": nrm(ks[24], (L, D_MODEL, D_MODEL), D_MODEL ** -0.5),
        "final_g": 1.0 + nrm(ks[25], (D_MODEL,), 0.02),
    }


def reference(x, norm_g, w_in, b_merge, mu_shift, w0, w_w2, a0, a_w2, v0, v_w1, v_w2,
              k_k, k_a, r_k, lnx_g, lnx_b, sgu_ln_g, sgu_ln_b, sgu_w, sgu_b,
              lb_logits, hgrn_g, w_branch, w_out, final_g):
    Bn, S, _ = x.shape
    lb_all = jnp.cumsum(jax.nn.softmax(lb_logits.astype(jnp.float32), axis=0), axis=0)
    lb_all = lb_all - lb_all[0]
    v_first = None
    for l in range(DEPTH):
        xn = _rmsnorm(x, norm_g[l])
        h = xn @ w_in[l]
        hA, hR = jnp.split(h, [N_SHIFT], axis=-1)
        zA, uB, vB, zB, qC, fC, iC, zC, gl = _split(hR, SPLIT_REST)
        vres = None if l == 0 else (v0[l - 1], v_w1[l - 1], v_w2[l - 1])
        oA, v_first = _rwkv7_branch(hA, zA, xn, v_first, mu_shift[l], w0[l], w_w2[l],
                                    a0[l], a_w2[l], vres, k_k[l], k_a[l], r_k[l],
                                    lnx_g[l], lnx_b[l])
        oB = _sgu_branch(uB, vB, zB, sgu_ln_g[l], sgu_ln_b[l], sgu_w[l], sgu_b[l])
        oC = _hgrn2_branch(qC, fC, iC, zC, lb_all[l], hgrn_g[l])
        branches = jnp.stack([oA, oB, oC], axis=2)
        proj = jnp.einsum('bsnw,nwd->bsnd', branches, w_branch[l])
        gates = jax.nn.sigmoid(gl.reshape(Bn, S, N_BRANCH, D_MODEL) + b_merge[l])
        y = jnp.sum(gates * proj, axis=2)
        x = x + y @ w_out[l]
    return _rmsnorm(x, final_g)
```

```python
import functools
import math

import jax
import jax.numpy as jnp
from jax import lax
from jax.experimental import pallas as pl
from jax.experimental.pallas import tpu as pltpu

F32 = jnp.float32
BF16 = jnp.bfloat16

D_MODEL = 1024
MIX_W = 1024
N_BRANCH = 3
CHUNK = 64
RWKV_HEAD_W = 64
R_LORA = 64
R_V = 32
SGU_BLOCK = 128
SGU_GROUPS = 8
HGRN_HEAD = 128
HGRN_HEADS = MIX_W // HGRN_HEAD
HGRN_STEP = 16
HGRN_GROUP = 64
HGRN_TBLK = 256
EPS = 1e-6
GN_EPS = 64e-5
LANES = 128
N_PAIRS = MIX_W // LANES
N_MAIN = 14 * 1024
N_EXT = 256
VMEM_LIMIT = 48 * 1024 * 1024

SLAB_R, SLAB_K, SLAB_V, SLAB_ZA, SLAB_UB, SLAB_VB, SLAB_ZB = 0, 1, 2, 3, 4, 5, 6
SLAB_QC, SLAB_FC, SLAB_IC, SLAB_ZC, SLAB_GL = 7, 8, 9, 10, 11


def _dot(a, b):
    return jnp.dot(a, b, preferred_element_type=F32)


def _dot_nt(a, b):
    return lax.dot_general(a, b, (((1,), (1,)), ((), ())), preferred_element_type=F32)


def _dot_tn(a, b):
    return lax.dot_general(a, b, (((0,), (0,)), ((), ())), preferred_element_type=F32)


def _split_dot(sel, x, terms):
    acc = None
    rem = x
    for _ in range(terms):
        part = rem.astype(BF16)
        d = _dot(sel, part)
        acc = d if acc is None else acc + d
        rem = rem - part.astype(F32)
    return acc


def _split_dot_r(x, sel, terms):
    acc = None
    rem = x
    for _ in range(terms):
        part = rem.astype(BF16)
        d = _dot(part, sel)
        acc = d if acc is None else acc + d
        rem = rem - part.astype(F32)
    return acc


def _sigmoid(x):
    return jax.nn.sigmoid(x)


def _silu(x):
    return x * jax.nn.sigmoid(x)


def _proj_kernel(x_ref, g_ref, wm_ref, we_ref, hm_ref, he_ref, xn_ref):
    @pl.when(pl.program_id(1) == 0)
    def _():
        x = x_ref[...]
        ms = jnp.mean(x * x, axis=-1, keepdims=True)
        xn = (x * lax.rsqrt(ms + EPS) * g_ref[...]).astype(BF16)
        xn_ref[...] = xn
        he_ref[...] = _dot(xn, we_ref[...])

    hm_ref[...] = _dot(xn_ref[...], wm_ref[...])


def _proj_in(x2, g, wm, we, tm=1024, tn=1024):
    m = x2.shape[0]
    tm = min(tm, m)
    return pl.pallas_call(
        _proj_kernel,
        grid=(m // tm, N_MAIN // tn),
        in_specs=[
            pl.BlockSpec((tm, D_MODEL), lambda i, j: (i, 0)),
            pl.BlockSpec((1, D_MODEL), lambda i, j: (0, 0)),
            pl.BlockSpec((D_MODEL, tn), lambda i, j: (0, j)),
            pl.BlockSpec((D_MODEL, N_EXT), lambda i, j: (0, 0)),
        ],
        out_specs=[
            pl.BlockSpec((tm, tn), lambda i, j: (i, j)),
            pl.BlockSpec((tm, N_EXT), lambda i, j: (i, 0)),
        ],
        out_shape=[
            jax.ShapeDtypeStruct((m, N_MAIN), F32),
            jax.ShapeDtypeStruct((m, N_EXT), F32),
        ],
        scratch_shapes=[pltpu.VMEM((tm, D_MODEL), BF16)],
        compiler_params=pltpu.CompilerParams(
            dimension_semantics=("parallel", "arbitrary"), vmem_limit_bytes=VMEM_LIMIT),
        name="proj_in",
    )(x2, g, wm, we)


def _rwkv_kernel(has_vres, *refs):
    if has_vres:
        (r_ref, k_ref, v_ref, z_ref, wa_ref, vl_ref, vf_ref, pv_ref, w2_ref, vw2_ref,
         o_ref, state_ref, cr_ref, ck_ref, cv_ref, cwa_ref) = refs
    else:
        (r_ref, k_ref, v_ref, z_ref, wa_ref, pv_ref, w2_ref,
         o_ref, vfo_ref, state_ref, cr_ref, ck_ref, cv_ref, cwa_ref) = refs

    @pl.when(pl.program_id(1) == 0)
    def _():
        state_ref[...] = jnp.zeros_like(state_ref)
        cr_ref[...] = jnp.zeros_like(cr_ref)
        ck_ref[...] = jnp.zeros_like(ck_ref)
        cv_ref[...] = jnp.zeros_like(cv_ref)
        cwa_ref[...] = jnp.zeros_like(cwa_ref)

    row = lax.broadcasted_iota(jnp.int32, (CHUNK, 1), 0)

    def shift_mix(x_ref, c_ref, mu):
        x = x_ref[...]
        prev = c_ref[7:8, :]
        xs = jnp.where(row == 0, prev, pltpu.roll(x, 1, 0))
        c_ref[...] = x[CHUNK - 8:CHUNK, :]
        return x + mu * (xs - x)

    pv = pv_ref[...]
    mu_r, mu_k, mu_v = pv[0:1], pv[1:2], pv[2:3]
    w0, a0, k_k, k_a, r_k = pv[3:4], pv[4:5], pv[5:6], pv[6:7], pv[7:8]
    lnx_g, lnx_b, v0 = pv[8:9], pv[9:10], pv[10:11]
    mu_wa = pv[11:12, 0:LANES]

    r = shift_mix(r_ref, cr_ref, mu_r)
    k = shift_mix(k_ref, ck_ref, mu_k)
    v = shift_mix(v_ref, cv_ref, mu_v)
    wa = shift_mix(wa_ref, cwa_ref, mu_wa)

    lane = lax.broadcasted_iota(jnp.int32, (1, LANES), 1)
    head_a = lane < RWKV_HEAD_W
    tl = jnp.where(head_a, jnp.tanh(wa), wa).astype(BF16)
    pre = _dot(tl, w2_ref[...])
    lw = -math.exp(-0.5) * _sigmoid(w0 + pre[:, :MIX_W])
    a = _sigmoid(a0 + pre[:, MIX_W:])

    if has_vres:
        gate = _sigmoid(v0 + _dot(vl_ref[...].astype(BF16), vw2_ref[...]))
        v = v + (vf_ref[...] - v) * gate
    else:
        vfo_ref[...] = v

    ti = lax.broadcasted_iota(jnp.int32, (CHUNK, CHUNK), 0)
    si = lax.broadcasted_iota(jnp.int32, (CHUNK, CHUNK), 1)
    tri = jnp.where(si <= ti, 1.0, 0.0).astype(BF16)
    cum = _split_dot(tri, lw, 3)
    cl = cum[CHUNK - 1:CHUNK]
    ecum = jnp.exp(cum)
    ecum_prev = jnp.exp(cum - lw)
    einv = jnp.exp(-cum)
    erest = jnp.exp(cl - cum)
    g_c = jnp.exp(cl)

    i2 = lax.broadcasted_iota(jnp.int32, (2 * CHUNK, 2 * CHUNK), 0)
    j2 = lax.broadcasted_iota(jnp.int32, (2 * CHUNK, 2 * CHUNK), 1)
    tok_i = jnp.bitwise_and(i2, CHUNK - 1)
    tok_j = jnp.bitwise_and(j2, CHUNK - 1)
    strict = tok_i > tok_j
    incl = tok_i >= tok_j
    eye = jnp.where(i2 == j2, 1.0, 0.0).astype(F32)
    li = lax.broadcasted_iota(jnp.int32, (LANES, LANES), 0)
    lj = lax.broadcasted_iota(jnp.int32, (LANES, LANES), 1)
    seg = jnp.where((li < RWKV_HEAD_W) == (lj < RWKV_HEAD_W), 1.0, 0.0).astype(BF16)

    def stack(x):
        return jnp.concatenate([jnp.where(head_a, x, 0.0), jnp.where(head_a, 0.0, x)], axis=0)

    def segsum(x):
        return _split_dot_r(x, seg, 2)

    for p in range(N_PAIRS):
        sl = slice(p * LANES, (p + 1) * LANES)
        rp, kp, vp, ap = r[:, sl], k[:, sl], v[:, sl], a[:, sl]
        kk = kp * k_k[:, sl]
        kk = kk * lax.rsqrt(jnp.maximum(segsum(kk * kk), 1e-24))
        k2 = kp * (1.0 + (ap - 1.0) * k_a[:, sl])
        kka = kk * ap
        at = -kk * ecum_prev[:, sl]
        bt = kka * einv[:, sl]
        kt = k2 * einv[:, sl]
        rt = rp * ecum[:, sl]
        bh = kka * erest[:, sl]
        kh = k2 * erest[:, sl]

        lhs = jnp.concatenate([stack(at), stack(rt)], axis=0).astype(BF16)
        rhs = jnp.concatenate([stack(bt), stack(kt)], axis=0).astype(BF16)
        g = _dot_nt(lhs, rhs)
        n_ab = jnp.where(strict, g[0:128, 0:128], 0.0)
        g_ak = jnp.where(strict, g[0:128, 128:256], 0.0)
        g_rb = jnp.where(incl, g[128:256, 0:128], 0.0)
        g_rk = jnp.where(incl, g[128:256, 128:256], 0.0)

        tinv = eye + n_ab
        pw = n_ab
        for _ in range(5):
            pb = pw.astype(BF16)
            pw = _dot(pb, pb)
            tinv = tinv + _dot(tinv.astype(BF16), pw.astype(BF16))

        s0 = state_ref[p]
        ah = _dot_nt(lhs, s0.astype(BF16))
        v_st = stack(vp)
        rhs_u = ah[0:128] + _dot(g_ak.astype(BF16), v_st.astype(BF16))
        u = _dot(tinv.astype(BF16), rhs_u.astype(BF16))
        uv = jnp.concatenate([u, v_st], axis=0).astype(BF16)
        y_st = ah[128:256] + _dot(jnp.concatenate([g_rb, g_rk], axis=1).astype(BF16), uv)
        y = y_st[0:CHUNK] + y_st[CHUNK:2 * CHUNK]
        bk = jnp.concatenate([stack(bh), stack(kh)], axis=0).astype(BF16)
        state_ref[p] = s0 * g_c[:, sl] + _dot_tn(uv, bk)

        mean = segsum(y) * (1.0 / RWKV_HEAD_W)
        d = y - mean
        var = segsum(d * d) * (1.0 / RWKV_HEAD_W)
        yn = d * lax.rsqrt(var + GN_EPS) * lnx_g[:, sl] + lnx_b[:, sl]
        bonus = segsum(rp * k2 * r_k[:, sl]) * vp
        o_ref[:, sl] = (yn + bonus) * _silu(z_ref[:, sl])


def _rwkv(hm, he, vfirst, pvec, w2, vw2, batch, seq):
    nt = seq // CHUNK
    has_vres = vfirst is not None
    tok = lambda b, t: b * nt + t
    slab = lambda s: pl.BlockSpec((CHUNK, MIX_W), lambda b, t: (tok(b, t), s))
    ext = lambda s: pl.BlockSpec((CHUNK, LANES), lambda b, t: (tok(b, t), s))
    full = lambda a: pl.BlockSpec(a.shape, lambda b, t: (0,) * a.ndim)
    act = pl.BlockSpec((CHUNK, MIX_W), lambda b, t: (tok(b, t), 0))
    in_specs = [slab(SLAB_R), slab(SLAB_K), slab(SLAB_V), slab(SLAB_ZA), ext(0)]
    args = [hm, hm, hm, hm, he]
    if has_vres:
        in_specs += [ext(1), act, full(pvec), full(w2), full(vw2)]
        args += [he, vfirst, pvec, w2, vw2]
        out_specs = act
        out_shape = jax.ShapeDtypeStruct((batch * seq, MIX_W), F32)
    else:
        in_specs += [full(pvec), full(w2)]
        args += [pvec, w2]
        out_specs = [act, act]
        out_shape = [jax.ShapeDtypeStruct((batch * seq, MIX_W), F32)] * 2
    return pl.pallas_call(
        functools.partial(_rwkv_kernel, has_vres),
        grid=(batch, nt),
        in_specs=in_specs,
        out_specs=out_specs,
        out_shape=out_shape,
        scratch_shapes=[
            pltpu.VMEM((N_PAIRS, LANES, LANES), F32),
            pltpu.VMEM((8, MIX_W), F32),
            pltpu.VMEM((8, MIX_W), F32),
            pltpu.VMEM((8, MIX_W), F32),
            pltpu.VMEM((8, LANES), F32),
        ],
        compiler_params=pltpu.CompilerParams(
            dimension_semantics=("parallel", "arbitrary"), vmem_limit_bytes=VMEM_LIMIT),
        name="rwkv",
    )(*args)


def _sgu_kernel(u_ref, v_ref, z_ref, lng_ref, lnb_ref, ws_ref, bias_ref, o_ref):
    v = v_ref[...]
    mean = jnp.mean(v, axis=-1, keepdims=True)
    d = v - mean
    var = jnp.mean(d * d, axis=-1, keepdims=True)
    vn = (d * lax.rsqrt(var + EPS) * lng_ref[...] + lnb_ref[...]).astype(BF16)
    ti = lax.broadcasted_iota(jnp.int32, (SGU_BLOCK, SGU_BLOCK), 0)
    si = lax.broadcasted_iota(jnp.int32, (SGU_BLOCK, SGU_BLOCK), 1)
    causal = (ti >= CHUNK) | (si < CHUNK)
    for g in range(SGU_GROUPS):
        sl = slice(g * LANES, (g + 1) * LANES)
        ws = jnp.where(causal, ws_ref[g], 0.0).astype(BF16)
        mixed = _dot(ws, vn[:, sl]) + bias_ref[:, sl]
        o_ref[:, sl] = u_ref[:, sl] * mixed * _silu(z_ref[:, sl])


def _sgu(hm, ln_g, ln_b, ws, bias, batch, seq):
    nb = seq // SGU_BLOCK
    slab = lambda s: pl.BlockSpec((SGU_BLOCK, MIX_W), lambda b, t: (b * nb + t, s))
    full = lambda a: pl.BlockSpec(a.shape, lambda b, t: (0,) * a.ndim)
    return pl.pallas_call(
        _sgu_kernel,
        grid=(batch, nb),
        in_specs=[slab(SLAB_UB), slab(SLAB_VB), slab(SLAB_ZB),
                  full(ln_g), full(ln_b), full(ws), full(bias)],
        out_specs=pl.BlockSpec((SGU_BLOCK, MIX_W), lambda b, t: (b * nb + t, 0)),
        out_shape=jax.ShapeDtypeStruct((batch * seq, MIX_W), F32),
        compiler_params=pltpu.CompilerParams(
            dimension_semantics=("parallel", "parallel"), vmem_limit_bytes=VMEM_LIMIT),
        name="sgu",
    )(hm, hm, hm, ln_g, ln_b, ws, bias)


def _hgrn_kernel(q_ref, f_ref, i_ref, z_ref, lb_ref, g_ref, o_ref, state_ref):
    @pl.when(pl.program_id(2) == 0)
    def _():
        state_ref[...] = jnp.zeros_like(state_ref)

    lb = lb_ref[...]
    gi = lax.broadcasted_iota(jnp.int32, (HGRN_GROUP, HGRN_GROUP), 0)
    gj = lax.broadcasted_iota(jnp.int32, (HGRN_GROUP, HGRN_GROUP), 1)
    same_step = jnp.right_shift(gi, 4) == jnp.right_shift(gj, 4)
    tri = jnp.where(same_step & (gj <= gi), 1.0, 0.0).astype(BF16)
    ones = jnp.ones((HGRN_HEAD, HGRN_HEAD), BF16)
    trow = lax.broadcasted_iota(jnp.int32, (HGRN_STEP, 1), 0)

    def group(gidx, carry):
        base = pl.multiple_of(gidx * HGRN_GROUP, HGRN_GROUP)
        rows = pl.ds(base, HGRN_GROUP)
        qc = q_ref[rows, :]
        q = _silu(qc)
        fg = lb + (1.0 - lb) * _sigmoid(f_ref[rows, :])
        kx = 1.0 - fg
        ic = i_ref[rows, :]
        bcs = _split_dot(tri, jnp.log(fg), 3)
        st = state_ref[...]
        outs = []
        for j in range(HGRN_GROUP // HGRN_STEP):
            rs = slice(j * HGRN_STEP, (j + 1) * HGRN_STEP)
            b, qs, ks, iv = bcs[rs], q[rs], kx[rs], ic[rs]
            bl = b[HGRN_STEP - 1:HGRN_STEP]
            o_inter = _dot_nt((qs * jnp.exp(b)).astype(BF16), st.astype(BF16))
            ps = []
            for s in range(HGRN_STEP):
                e = jnp.exp(jnp.minimum(b - b[s:s + 1], 0.0))
                ps.append(jnp.where(trow >= s, qs * ks[s:s + 1] * e, 0.0))
            att = _dot(jnp.concatenate(ps, axis=0).astype(BF16), ones)
            o_d = att[0:HGRN_STEP] * iv[0:1]
            for s in range(1, HGRN_STEP):
                o_d = o_d + att[s * HGRN_STEP:(s + 1) * HGRN_STEP] * iv[s:s + 1]
            outs.append(o_inter + o_d)
            st = st * jnp.exp(bl) + _dot_tn(iv.astype(BF16), (ks * jnp.exp(bl - b)).astype(BF16))
        state_ref[...] = st
        o = jnp.concatenate(outs, axis=0)
        o = o * lax.rsqrt(jnp.mean(o * o, axis=-1, keepdims=True) + EPS) * g_ref[...]
        o_ref[rows, :] = o * _silu(z_ref[rows, :])
        return carry

    lax.fori_loop(0, HGRN_TBLK // HGRN_GROUP, group, 0)


def _hgrn(hm, lb, g_norm, batch, seq):
    nt = seq // HGRN_TBLK
    col = lambda s: pl.BlockSpec(
        (HGRN_TBLK, HGRN_HEAD), lambda b, h, t: (b * nt + t, s * HGRN_HEADS + h))
    vec = pl.BlockSpec((1, HGRN_HEAD), lambda b, h, t: (0, h))
    return pl.pallas_call(
        _hgrn_kernel,
        grid=(batch, HGRN_HEADS, nt),
        in_specs=[col(SLAB_QC), col(SLAB_FC), col(SLAB_IC), col(SLAB_ZC), vec, vec],
        out_specs=pl.BlockSpec((HGRN_TBLK, HGRN_HEAD), lambda b, h, t: (b * nt + t, h)),
        out_shape=jax.ShapeDtypeStruct((batch * seq, MIX_W), F32),
        scratch_shapes=[pltpu.VMEM((HGRN_HEAD, HGRN_HEAD), F32)],
        compiler_params=pltpu.CompilerParams(
            dimension_semantics=("parallel", "parallel", "arbitrary"),
            vmem_limit_bytes=VMEM_LIMIT),
        name="hgrn",
    )(hm, hm, hm, hm, lb, g_norm)


def _merge_kernel(final, *refs):
    if final:
        oa, ob, oc, g0, g1, g2, x_ref, bm_ref, wb_ref, wo_ref, fg_ref, out_ref = refs
    else:
        oa, ob, oc, g0, g1, g2, x_ref, bm_ref, wb_ref, wo_ref, out_ref = refs
    y = None
    for n, (o_ref, gl_ref) in enumerate(((oa, g0), (ob, g1), (oc, g2))):
        proj = _dot(o_ref[...].astype(BF16), wb_ref[n])
        term = _sigmoid(gl_ref[...] + bm_ref[n:n + 1, :]) * proj
        y = term if y is None else y + term
    xo = x_ref[...] + _dot(y.astype(BF16), wo_ref[...])
    if final:
        ms = jnp.mean(xo * xo, axis=-1, keepdims=True)
        xo = xo * lax.rsqrt(ms + EPS) * fg_ref[...]
    out_ref[...] = xo


def _merge(oa, ob, oc, hm, x2, bm, wb, wo, final_g, tm=256):
    m = x2.shape[0]
    tm = min(tm, m)
    final = final_g is not None
    act = pl.BlockSpec((tm, D_MODEL), lambda i: (i, 0))
    gl = lambda n: pl.BlockSpec((tm, D_MODEL), lambda i: (i, SLAB_GL + n))
    full = lambda a: pl.BlockSpec(a.shape, lambda i: (0,) * a.ndim)
    in_specs = [act, act, act, gl(0), gl(1), gl(2), act, full(bm), full(wb), full(wo)]
    args = [oa, ob, oc, hm, hm, hm, x2, bm, wb, wo]
    if final:
        in_specs.append(full(final_g))
        args.append(final_g)
    return pl.pallas_call(
        functools.partial(_merge_kernel, final),
        grid=(m // tm,),
        in_specs=in_specs,
        out_specs=act,
        out_shape=jax.ShapeDtypeStruct((m, D_MODEL), F32),
        compiler_params=pltpu.CompilerParams(
            dimension_semantics=("parallel",), vmem_limit_bytes=VMEM_LIMIT),
        name="merge",
    )(*args)


def kernel(x, norm_g, w_in, b_merge, mu_shift, w0, w_w2, a0, a_w2, v0, v_w1, v_w2, k_k, k_a, r_k,
           lnx_g, lnx_b, sgu_ln_g, sgu_ln_b, sgu_w, sgu_b, lb_logits, hgrn_g, w_branch, w_out,
           final_g):
    batch, seq, _ = x.shape
    depth = w_in.shape[0]
    assert seq % HGRN_TBLK == 0 and x.shape[2] == D_MODEL
    n_shift = 3 * MIX_W + 2 * R_LORA

    lb_all = jnp.cumsum(jax.nn.softmax(lb_logits.astype(F32), axis=0), axis=0)
    lb_all = lb_all - lb_all[0]

    x2 = x.reshape(batch * seq, D_MODEL)
    vfirst = None
    for l in range(depth):
        w = w_in[l]
        wm = jnp.concatenate([w[:, :3 * MIX_W], w[:, n_shift:]], axis=1).astype(BF16)
        ext = [w[:, 3 * MIX_W:n_shift]]
        if l > 0:
            ext.append(v_w1[l - 1])
        we = jnp.concatenate(ext, axis=1)
        we = jnp.pad(we, ((0, 0), (0, N_EXT - we.shape[1]))).astype(BF16)

        zrow = jnp.zeros((MIX_W,), F32)
        mu = mu_shift[l]
        rows = [mu[0:MIX_W], mu[MIX_W:2 * MIX_W], mu[2 * MIX_W:3 * MIX_W], w0[l], a0[l], k_k[l],
                k_a[l], r_k[l], lnx_g[l], lnx_b[l],
                v0[l - 1] if l > 0 else zrow,
                jnp.pad(mu[3 * MIX_W:n_shift], (0, MIX_W - 2 * R_LORA))]
        pvec = jnp.stack(rows + [zrow] * (16 - len(rows)), axis=0)
        zl = jnp.zeros((R_LORA, MIX_W), F32)
        w2 = jnp.concatenate(
            [jnp.concatenate([w_w2[l], zl], axis=0), jnp.concatenate([zl, a_w2[l]], axis=0)],
            axis=1).astype(BF16)

        hm, he = _proj_in(x2, norm_g[l][None, :], wm, we)

        if l == 0:
            oa, vfirst = _rwkv(hm, he, None, pvec, w2, None, batch, seq)
        else:
            vw2 = jnp.pad(v_w2[l - 1], ((0, LANES - R_V), (0, 0))).astype(BF16)
            oa = _rwkv(hm, he, vfirst, pvec, w2, vw2, batch, seq)

        bias = jnp.repeat(sgu_b[l].T, LANES, axis=1)
        ob = _sgu(hm, sgu_ln_g[l][None, :], sgu_ln_b[l][None, :], sgu_w[l], bias, batch, seq)
        oc = _hgrn(hm, lb_all[l][None, :], hgrn_g[l][None, :], batch, seq)

        bm = jnp.pad(b_merge[l], ((0, 8 - N_BRANCH), (0, 0)))
        x2 = _merge(oa, ob, oc, hm, x2, bm, w_branch[l].astype(BF16), w_out[l].astype(BF16),
                    final_g[None, :] if l == depth - 1 else None)
    return x2.reshape(batch, seq, D_MODEL)
```

```python
import functools
import math

import jax
import jax.numpy as jnp
from jax import lax
from jax.experimental import pallas as pl
from jax.experimental.pallas import tpu as pltpu

F32 = jnp.float32
BF16 = jnp.bfloat16

D_MODEL = 1024
MIX_W = 1024
N_BRANCH = 3
CHUNK = 64
RWKV_HEAD_W = 64
R_LORA = 64
R_V = 32
SGU_BLOCK = 128
SGU_GROUPS = 8
HGRN_HEAD = 128
HGRN_HEADS = MIX_W // HGRN_HEAD
HGRN_STEP = 16
HGRN_GROUP = 64
HGRN_TBLK = 256
EPS = 1e-6
GN_EPS = 64e-5
LANES = 128
N_PAIRS = MIX_W // LANES
N_MAIN = 14 * 1024
N_EXT = 256
VMEM_LIMIT = 48 * 1024 * 1024

SLAB_R, SLAB_K, SLAB_V, SLAB_ZA, SLAB_UB, SLAB_VB, SLAB_ZB = 0, 1, 2, 3, 4, 5, 6
SLAB_QC, SLAB_FC, SLAB_IC, SLAB_ZC, SLAB_GL = 7, 8, 9, 10, 11


def _dot(a, b):
    return jnp.dot(a, b, preferred_element_type=F32)


def _dot_nt(a, b):
    return lax.dot_general(a, b, (((1,), (1,)), ((), ())), preferred_element_type=F32)


def _dot_tn(a, b):
    return lax.dot_general(a, b, (((0,), (0,)), ((), ())), preferred_element_type=F32)


def _split_dot(sel, x, terms):
    acc = None
    rem = x
    for _ in range(terms):
        part = rem.astype(BF16)
        d = _dot(sel, part)
        acc = d if acc is None else acc + d
        rem = rem - part.astype(F32)
    return acc


def _split_dot_r(x, sel, terms):
    acc = None
    rem = x
    for _ in range(terms):
        part = rem.astype(BF16)
        d = _dot(part, sel)
        acc = d if acc is None else acc + d
        rem = rem - part.astype(F32)
    return acc


def _sigmoid(x):
    return jax.nn.sigmoid(x)


def _silu(x):
    return x * jax.nn.sigmoid(x)


def _proj_kernel(x_ref, g_ref, wm_ref, we_ref, hm_ref, he_ref, xn_ref):
    @pl.when(pl.program_id(1) == 0)
    def _():
        x = x_ref[...]
        ms = jnp.mean(x * x, axis=-1, keepdims=True)
        xn = (x * lax.rsqrt(ms + EPS) * g_ref[...]).astype(BF16)
        xn_ref[...] = xn
        he_ref[...] = _dot(xn, we_ref[...])

    hm_ref[...] = _dot(xn_ref[...], wm_ref[...])


def _proj_in(x2, g, wm, we, tm=1024, tn=1024):
    m = x2.shape[0]
    tm = min(tm, m)
    return pl.pallas_call(
        _proj_kernel,
        grid=(m // tm, N_MAIN // tn),
        in_specs=[
            pl.BlockSpec((tm, D_MODEL), lambda i, j: (i, 0)),
            pl.BlockSpec((1, D_MODEL), lambda i, j: (0, 0)),
            pl.BlockSpec((D_MODEL, tn), lambda i, j: (0, j)),
            pl.BlockSpec((D_MODEL, N_EXT), lambda i, j: (0, 0)),
        ],
        out_specs=[
            pl.BlockSpec((tm, tn), lambda i, j: (i, j)),
            pl.BlockSpec((tm, N_EXT), lambda i, j: (i, 0)),
        ],
        out_shape=[
            jax.ShapeDtypeStruct((m, N_MAIN), F32),
            jax.ShapeDtypeStruct((m, N_EXT), F32),
        ],
        scratch_shapes=[pltpu.VMEM((tm, D_MODEL), BF16)],
        compiler_params=pltpu.CompilerParams(
            dimension_semantics=("parallel", "arbitrary"), vmem_limit_bytes=VMEM_LIMIT),
        name="proj_in",
    )(x2, g, wm, we)


def _rwkv_kernel(has_vres, *refs):
    if has_vres:
        (r_ref, k_ref, v_ref, z_ref, wa_ref, vl_ref, vf_ref, pv_ref, w2_ref, vw2_ref,
         o_ref, state_ref, cr_ref, ck_ref, cv_ref, cwa_ref) = refs
    else:
        (r_ref, k_ref, v_ref, z_ref, wa_ref, pv_ref, w2_ref,
         o_ref, vfo_ref, state_ref, cr_ref, ck_ref, cv_ref, cwa_ref) = refs

    @pl.when(pl.program_id(1) == 0)
    def _():
        state_ref[...] = jnp.zeros_like(state_ref)
        cr_ref[...] = jnp.zeros_like(cr_ref)
        ck_ref[...] = jnp.zeros_like(ck_ref)
        cv_ref[...] = jnp.zeros_like(cv_ref)
        cwa_ref[...] = jnp.zeros_like(cwa_ref)

    row = lax.broadcasted_iota(jnp.int32, (CHUNK, 1), 0)

    def shift_mix(x_ref, c_ref, mu):
        x = x_ref[...]
        prev = c_ref[7:8, :]
        xs = jnp.where(row == 0, prev, pltpu.roll(x, 1, 0))
        c_ref[...] = x[CHUNK - 8:CHUNK, :]
        return x + mu * (xs - x)

    pv = pv_ref[...]
    mu_r, mu_k, mu_v = pv[0:1], pv[1:2], pv[2:3]
    w0, a0, k_k, k_a, r_k = pv[3:4], pv[4:5], pv[5:6], pv[6:7], pv[7:8]
    lnx_g, lnx_b, v0 = pv[8:9], pv[9:10], pv[10:11]
    mu_wa = pv[11:12, 0:LANES]

    r = shift_mix(r_ref, cr_ref, mu_r)
    k = shift_mix(k_ref, ck_ref, mu_k)
    v = shift_mix(v_ref, cv_ref, mu_v)
    wa = shift_mix(wa_ref, cwa_ref, mu_wa)

    lane = lax.broadcasted_iota(jnp.int32, (1, LANES), 1)
    head_a = lane < RWKV_HEAD_W
    tl = jnp.where(head_a, jnp.tanh(wa), wa).astype(BF16)
    pre = _dot(tl, w2_ref[...])
    lw = -math.exp(-0.5) * _sigmoid(w0 + pre[:, :MIX_W])
    a = _sigmoid(a0 + pre[:, MIX_W:])

    if has_vres:
        gate = _sigmoid(v0 + _dot(vl_ref[...].astype(BF16), vw2_ref[...]))
        v = v + (vf_ref[...] - v) * gate
    else:
        vfo_ref[...] = v

    ti = lax.broadcasted_iota(jnp.int32, (CHUNK, CHUNK), 0)
    si = lax.broadcasted_iota(jnp.int32, (CHUNK, CHUNK), 1)
    tri = jnp.where(si <= ti, 1.0, 0.0).astype(BF16)
    cum = _split_dot(tri, lw, 3)
    cl = cum[CHUNK - 1:CHUNK]
    ecum = jnp.exp(cum)
    ecum_prev = jnp.exp(cum - lw)
    einv = jnp.exp(-cum)
    erest = jnp.exp(cl - cum)
    g_c = jnp.exp(cl)

    i2 = lax.broadcasted_iota(jnp.int32, (2 * CHUNK, 2 * CHUNK), 0)
    j2 = lax.broadcasted_iota(jnp.int32, (2 * CHUNK, 2 * CHUNK), 1)
    tok_i = jnp.bitwise_and(i2, CHUNK - 1)
    tok_j = jnp.bitwise_and(j2, CHUNK - 1)
    strict = tok_i > tok_j
    incl = tok_i >= tok_j
    eye = jnp.where(i2 == j2, 1.0, 0.0).astype(F32)
    li = lax.broadcasted_iota(jnp.int32, (LANES, LANES), 0)
    lj = lax.broadcasted_iota(jnp.int32, (LANES, LANES), 1)
    seg = jnp.where((li < RWKV_HEAD_W) == (lj < RWKV_HEAD_W), 1.0, 0.0).astype(BF16)

    def stack(x):
        return jnp.concatenate([jnp.where(head_a, x, 0.0), jnp.where(head_a, 0.0, x)], axis=0)

    def segsum(x):
        return _split_dot_r(x, seg, 2)

    pairs = range(N_PAIRS)
    sls = [slice(p * LANES, (p + 1) * LANES) for p in pairs]
    k2s, lhss, rhss, bks, v_sts = [], [], [], [], []
    kk_raw = [k[:, sl] * k_k[:, sl] for sl in sls]
    kk_ss = [segsum(kk * kk) for kk in kk_raw]
    for p, sl in enumerate(sls):
        kp, ap = k[:, sl], a[:, sl]
        kk = kk_raw[p] * lax.rsqrt(jnp.maximum(kk_ss[p], 1e-24))
        k2 = kp * (1.0 + (ap - 1.0) * k_a[:, sl])
        kka = kk * ap
        at = -kk * ecum_prev[:, sl]
        bt = kka * einv[:, sl]
        kt = k2 * einv[:, sl]
        rt = r[:, sl] * ecum[:, sl]
        bh = kka * erest[:, sl]
        kh = k2 * erest[:, sl]
        k2s.append(k2)
        lhss.append(jnp.concatenate([stack(at), stack(rt)], axis=0).astype(BF16))
        rhss.append(jnp.concatenate([stack(bt), stack(kt)], axis=0).astype(BF16))
        bks.append(jnp.concatenate([stack(bh), stack(kh)], axis=0).astype(BF16))
        v_sts.append(stack(v[:, sl]))

    gs = [_dot_nt(lhss[p], rhss[p]) for p in pairs]
    s0s = [state_ref[p] for p in pairs]
    ahs = [_dot_nt(lhss[p], s0s[p].astype(BF16)) for p in pairs]
    n_abs = [jnp.where(strict, g[0:128, 0:128], 0.0) for g in gs]
    g_aks = [jnp.where(strict, g[0:128, 128:256], 0.0).astype(BF16) for g in gs]
    g_rs = [jnp.concatenate([jnp.where(incl, g[128:256, 0:128], 0.0),
                             jnp.where(incl, g[128:256, 128:256], 0.0)], axis=1).astype(BF16)
            for g in gs]
    rhs_us = [ahs[p][0:128] + _dot(g_aks[p], v_sts[p].astype(BF16)) for p in pairs]

    tinvs = [eye + n for n in n_abs]
    pws = n_abs
    for _ in range(5):
        pbs = [pw.astype(BF16) for pw in pws]
        pws = [_dot(pb, pb) for pb in pbs]
        tinvs = [tinvs[p] + _dot(tinvs[p].astype(BF16), pws[p].astype(BF16)) for p in pairs]

    us = [_dot(tinvs[p].astype(BF16), rhs_us[p].astype(BF16)) for p in pairs]
    uvs = [jnp.concatenate([us[p], v_sts[p]], axis=0).astype(BF16) for p in pairs]
    y_sts = [ahs[p][128:256] + _dot(g_rs[p], uvs[p]) for p in pairs]
    new_states = [s0s[p] * g_c[:, sls[p]] + _dot_tn(uvs[p], bks[p]) for p in pairs]

    bonuses = [segsum(r[:, sl] * k2s[p] * r_k[:, sl]) * v[:, sl] for p, sl in enumerate(sls)]
    ys = [y_st[0:CHUNK] + y_st[CHUNK:2 * CHUNK] for y_st in y_sts]
    means = [segsum(y) * (1.0 / RWKV_HEAD_W) for y in ys]
    ds = [ys[p] - means[p] for p in pairs]
    variances = [segsum(d * d) * (1.0 / RWKV_HEAD_W) for d in ds]
    outs = []
    for p, sl in enumerate(sls):
        yn = ds[p] * lax.rsqrt(variances[p] + GN_EPS) * lnx_g[:, sl] + lnx_b[:, sl]
        outs.append((yn + bonuses[p]) * _silu(z_ref[:, sl]))
    o_ref[...] = jnp.concatenate(outs, axis=1)
    state_ref[...] = jnp.stack(new_states, axis=0)


def _rwkv(hm, he, vfirst, pvec, w2, vw2, batch, seq):
    nt = seq // CHUNK
    has_vres = vfirst is not None
    tok = lambda b, t: b * nt + t
    slab = lambda s: pl.BlockSpec((CHUNK, MIX_W), lambda b, t: (tok(b, t), s))
    ext = lambda s: pl.BlockSpec((CHUNK, LANES), lambda b, t: (tok(b, t), s))
    full = lambda a: pl.BlockSpec(a.shape, lambda b, t: (0,) * a.ndim)
    act = pl.BlockSpec((CHUNK, MIX_W), lambda b, t: (tok(b, t), 0))
    in_specs = [slab(SLAB_R), slab(SLAB_K), slab(SLAB_V), slab(SLAB_ZA), ext(0)]
    args = [hm, hm, hm, hm, he]
    if has_vres:
        in_specs += [ext(1), act, full(pvec), full(w2), full(vw2)]
        args += [he, vfirst, pvec, w2, vw2]
        out_specs = act
        out_shape = jax.ShapeDtypeStruct((batch * seq, MIX_W), F32)
    else:
        in_specs += [full(pvec), full(w2)]
        args += [pvec, w2]
        out_specs = [act, act]
        out_shape = [jax.ShapeDtypeStruct((batch * seq, MIX_W), F32)] * 2
    return pl.pallas_call(
        functools.partial(_rwkv_kernel, has_vres),
        grid=(batch, nt),
        in_specs=in_specs,
        out_specs=out_specs,
        out_shape=out_shape,
        scratch_shapes=[
            pltpu.VMEM((N_PAIRS, LANES, LANES), F32),
            pltpu.VMEM((8, MIX_W), F32),
            pltpu.VMEM((8, MIX_W), F32),
            pltpu.VMEM((8, MIX_W), F32),
            pltpu.VMEM((8, LANES), F32),
        ],
        compiler_params=pltpu.CompilerParams(
            dimension_semantics=("parallel", "arbitrary"), vmem_limit_bytes=VMEM_LIMIT),
        name="rwkv",
    )(*args)


def _sgu_kernel(u_ref, v_ref, z_ref, lng_ref, lnb_ref, ws_ref, bias_ref, o_ref):
    v = v_ref[...]
    mean = jnp.mean(v, axis=-1, keepdims=True)
    d = v - mean
    var = jnp.mean(d * d, axis=-1, keepdims=True)
    vn = (d * lax.rsqrt(var + EPS) * lng_ref[...] + lnb_ref[...]).astype(BF16)
    ti = lax.broadcasted_iota(jnp.int32, (SGU_BLOCK, SGU_BLOCK), 0)
    si = lax.broadcasted_iota(jnp.int32, (SGU_BLOCK, SGU_BLOCK), 1)
    causal = (ti >= CHUNK) | (si < CHUNK)
    for g in range(SGU_GROUPS):
        sl = slice(g * LANES, (g + 1) * LANES)
        ws = jnp.where(causal, ws_ref[g], 0.0).astype(BF16)
        mixed = _dot(ws, vn[:, sl]) + bias_ref[:, sl]
        o_ref[:, sl] = u_ref[:, sl] * mixed * _silu(z_ref[:, sl])


def _sgu(hm, ln_g, ln_b, ws, bias, batch, seq):
    nb = seq // SGU_BLOCK
    slab = lambda s: pl.BlockSpec((SGU_BLOCK, MIX_W), lambda b, t: (b * nb + t, s))
    full = lambda a: pl.BlockSpec(a.shape, lambda b, t: (0,) * a.ndim)
    return pl.pallas_call(
        _sgu_kernel,
        grid=(batch, nb),
        in_specs=[slab(SLAB_UB), slab(SLAB_VB), slab(SLAB_ZB),
                  full(ln_g), full(ln_b), full(ws), full(bias)],
        out_specs=pl.BlockSpec((SGU_BLOCK, MIX_W), lambda b, t: (b * nb + t, 0)),
        out_shape=jax.ShapeDtypeStruct((batch * seq, MIX_W), F32),
        compiler_params=pltpu.CompilerParams(
            dimension_semantics=("parallel", "parallel"), vmem_limit_bytes=VMEM_LIMIT),
        name="sgu",
    )(hm, hm, hm, ln_g, ln_b, ws, bias)


def _hgrn_kernel(q_ref, f_ref, i_ref, z_ref, lb_ref, g_ref, o_ref, state_ref):
    @pl.when(pl.program_id(2) == 0)
    def _():
        state_ref[...] = jnp.zeros_like(state_ref)

    lb = lb_ref[...]
    gi = lax.broadcasted_iota(jnp.int32, (HGRN_TBLK, HGRN_TBLK), 0)
    gj = lax.broadcasted_iota(jnp.int32, (HGRN_TBLK, HGRN_TBLK), 1)
    same_step = jnp.right_shift(gi, 4) == jnp.right_shift(gj, 4)
    tri = jnp.where(same_step & (gj <= gi), 1.0, 0.0).astype(BF16)
    ones = jnp.ones((HGRN_HEAD, HGRN_HEAD), BF16)
    trow = lax.broadcasted_iota(jnp.int32, (HGRN_STEP, 1), 0)

    q = _silu(q_ref[...])
    fg = lb + (1.0 - lb) * _sigmoid(f_ref[...])
    kx = 1.0 - fg
    ic = i_ref[...]
    bcs = _split_dot(tri, jnp.log(fg), 3)
    steps = range(HGRN_TBLK // HGRN_STEP)
    rss = [slice(j * HGRN_STEP, (j + 1) * HGRN_STEP) for j in steps]
    bls = [bcs[rs][HGRN_STEP - 1:HGRN_STEP] for rs in rss]
    atts = []
    for rs in rss:
        b, qs, ks = bcs[rs], q[rs], kx[rs]
        ps = []
        for s in range(HGRN_STEP):
            e = jnp.exp(jnp.minimum(b - b[s:s + 1], 0.0))
            ps.append(jnp.where(trow >= s, qs * ks[s:s + 1] * e, 0.0))
        atts.append(_dot(jnp.concatenate(ps, axis=0).astype(BF16), ones))
    incs = [_dot_tn(ic[rs].astype(BF16), (kx[rs] * jnp.exp(bl - bcs[rs])).astype(BF16))
            for rs, bl in zip(rss, bls)]
    qb = (q * jnp.exp(bcs)).astype(BF16)
    st = state_ref[...]
    o_inters = []
    for j in steps:
        o_inters.append(_dot_nt(qb[rss[j]], st.astype(BF16)))
        st = st * jnp.exp(bls[j]) + incs[j]
    state_ref[...] = st
    outs = []
    for j, rs in enumerate(rss):
        att, iv = atts[j], ic[rs]
        o_d = att[0:HGRN_STEP] * iv[0:1]
        for s in range(1, HGRN_STEP):
            o_d = o_d + att[s * HGRN_STEP:(s + 1) * HGRN_STEP] * iv[s:s + 1]
        outs.append(o_inters[j] + o_d)
    o = jnp.concatenate(outs, axis=0)
    o = o * lax.rsqrt(jnp.mean(o * o, axis=-1, keepdims=True) + EPS) * g_ref[...]
    o_ref[...] = o * _silu(z_ref[...])


def _hgrn(hm, lb, g_norm, batch, seq):
    nt = seq // HGRN_TBLK
    col = lambda s: pl.BlockSpec(
        (HGRN_TBLK, HGRN_HEAD), lambda b, h, t: (b * nt + t, s * HGRN_HEADS + h))
    vec = pl.BlockSpec((1, HGRN_HEAD), lambda b, h, t: (0, h))
    return pl.pallas_call(
        _hgrn_kernel,
        grid=(batch, HGRN_HEADS, nt),
        in_specs=[col(SLAB_QC), col(SLAB_FC), col(SLAB_IC), col(SLAB_ZC), vec, vec],
        out_specs=pl.BlockSpec((HGRN_TBLK, HGRN_HEAD), lambda b, h, t: (b * nt + t, h)),
        out_shape=jax.ShapeDtypeStruct((batch * seq, MIX_W), F32),
        scratch_shapes=[pltpu.VMEM((HGRN_HEAD, HGRN_HEAD), F32)],
        compiler_params=pltpu.CompilerParams(
            dimension_semantics=("parallel", "parallel", "arbitrary"),
            vmem_limit_bytes=VMEM_LIMIT),
        name="hgrn",
    )(hm, hm, hm, hm, lb, g_norm)


def _merge_kernel(final, *refs):
    if final:
        oa, ob, oc, g0, g1, g2, x_ref, bm_ref, wb_ref, wo_ref, fg_ref, out_ref = refs
    else:
        oa, ob, oc, g0, g1, g2, x_ref, bm_ref, wb_ref, wo_ref, out_ref = refs
    y = None
    for n, (o_ref, gl_ref) in enumerate(((oa, g0), (ob, g1), (oc, g2))):
        proj = _dot(o_ref[...].astype(BF16), wb_ref[n])
        term = _sigmoid(gl_ref[...] + bm_ref[n:n + 1, :]) * proj
        y = term if y is None else y + term
    xo = x_ref[...] + _dot(y.astype(BF16), wo_ref[...])
    if final:
        ms = jnp.mean(xo * xo, axis=-1, keepdims=True)
        xo = xo * lax.rsqrt(ms + EPS) * fg_ref[...]
    out_ref[...] = xo


def _merge(oa, ob, oc, hm, x2, bm, wb, wo, final_g, tm=256):
    m = x2.shape[0]
    tm = min(tm, m)
    final = final_g is not None
    act = pl.BlockSpec((tm, D_MODEL), lambda i: (i, 0))
    gl = lambda n: pl.BlockSpec((tm, D_MODEL), lambda i: (i, SLAB_GL + n))
    full = lambda a: pl.BlockSpec(a.shape, lambda i: (0,) * a.ndim)
    in_specs = [act, act, act, gl(0), gl(1), gl(2), act, full(bm), full(wb), full(wo)]
    args = [oa, ob, oc, hm, hm, hm, x2, bm, wb, wo]
    if final:
        in_specs.append(full(final_g))
        args.append(final_g)
    return pl.pallas_call(
        functools.partial(_merge_kernel, final),
        grid=(m // tm,),
        in_specs=in_specs,
        out_specs=act,
        out_shape=jax.ShapeDtypeStruct((m, D_MODEL), F32),
        compiler_params=pltpu.CompilerParams(
            dimension_semantics=("parallel",), vmem_limit_bytes=VMEM_LIMIT),
        name="merge",
    )(*args)


def kernel(x, norm_g, w_in, b_merge, mu_shift, w0, w_w2, a0, a_w2, v0, v_w1, v_w2, k_k, k_a, r_k,
           lnx_g, lnx_b, sgu_ln_g, sgu_ln_b, sgu_w, sgu_b, lb_logits, hgrn_g, w_branch, w_out,
           final_g):
    batch, seq, _ = x.shape
    depth = w_in.shape[0]
    assert seq % HGRN_TBLK == 0 and x.shape[2] == D_MODEL
    n_shift = 3 * MIX_W + 2 * R_LORA

    lb_all = jnp.cumsum(jax.nn.softmax(lb_logits.astype(F32), axis=0), axis=0)
    lb_all = lb_all - lb_all[0]

    x2 = x.reshape(batch * seq, D_MODEL)
    vfirst = None
    for l in range(depth):
        w = w_in[l]
        wm = jnp.concatenate([w[:, :3 * MIX_W], w[:, n_shift:]], axis=1).astype(BF16)
        ext = [w[:, 3 * MIX_W:n_shift]]
        if l > 0:
            ext.append(v_w1[l - 1])
        we = jnp.concatenate(ext, axis=1)
        we = jnp.pad(we, ((0, 0), (0, N_EXT - we.shape[1]))).astype(BF16)

        zrow = jnp.zeros((MIX_W,), F32)
        mu = mu_shift[l]
        rows = [mu[0:MIX_W], mu[MIX_W:2 * MIX_W], mu[2 * MIX_W:3 * MIX_W], w0[l], a0[l], k_k[l],
                k_a[l], r_k[l], lnx_g[l], lnx_b[l],
                v0[l - 1] if l > 0 else zrow,
                jnp.pad(mu[3 * MIX_W:n_shift], (0, MIX_W - 2 * R_LORA))]
        pvec = jnp.stack(rows + [zrow] * (16 - len(rows)), axis=0)
        zl = jnp.zeros((R_LORA, MIX_W), F32)
        w2 = jnp.concatenate(
            [jnp.concatenate([w_w2[l], zl], axis=0), jnp.concatenate([zl, a_w2[l]], axis=0)],
            axis=1).astype(BF16)

        hm, he = _proj_in(x2, norm_g[l][None, :], wm, we)

        if l == 0:
            oa, vfirst = _rwkv(hm, he, None, pvec, w2, None, batch, seq)
        else:
            vw2 = jnp.pad(v_w2[l - 1], ((0, LANES - R_V), (0, 0))).astype(BF16)
            oa = _rwkv(hm, he, vfirst, pvec, w2, vw2, batch, seq)

        bias = jnp.repeat(sgu_b[l].T, LANES, axis=1)
        ob = _sgu(hm, sgu_ln_g[l][None, :], sgu_ln_b[l][None, :], sgu_w[l], bias, batch, seq)
        oc = _hgrn(hm, lb_all[l][None, :], hgrn_g[l][None, :], batch, seq)

        bm = jnp.pad(b_merge[l], ((0, 8 - N_BRANCH), (0, 0)))
        x2 = _merge(oa, ob, oc, hm, x2, bm, w_branch[l].astype(BF16), w_out[l].astype(BF16),
                    final_g[None, :] if l == depth - 1 else None)
    return x2.reshape(batch, seq, D_MODEL)
```

```python
import functools
import math

import jax
import jax.numpy as jnp
from jax import lax
from jax.experimental import pallas as pl
from jax.experimental.pallas import tpu as pltpu

F32 = jnp.float32
BF16 = jnp.bfloat16

D_MODEL = 1024
MIX_W = 1024
N_BRANCH = 3
CHUNK = 64
RWKV_TBLK = 128
RWKV_HEAD_W = 64
R_LORA = 64
R_V = 32
SGU_BLOCK = 128
SGU_GROUPS = 8
HGRN_HEAD = 128
HGRN_HEADS = MIX_W // HGRN_HEAD
HGRN_STEP = 16
HGRN_TBLK = 512
EPS = 1e-6
GN_EPS = 64e-5
LANES = 128
GROUP_W = 256
N_GROUPS = MIX_W // GROUP_W
N_MAIN = 14 * 1024
N_EXT = 256
VMEM_LIMIT = 48 * 1024 * 1024

SLAB_R, SLAB_K, SLAB_V, SLAB_ZA, SLAB_UB, SLAB_VB, SLAB_ZB = 0, 1, 2, 3, 4, 5, 6
SLAB_QC, SLAB_FC, SLAB_IC, SLAB_ZC, SLAB_GL = 7, 8, 9, 10, 11


def _dot(a, b):
    return jnp.dot(a, b, preferred_element_type=F32)


def _dot_nt(a, b):
    return lax.dot_general(a, b, (((1,), (1,)), ((), ())), preferred_element_type=F32)


def _dot_tn(a, b):
    return lax.dot_general(a, b, (((0,), (0,)), ((), ())), preferred_element_type=F32)


def _split_dot(sel, x, terms):
    acc = None
    rem = x
    for _ in range(terms):
        part = rem.astype(BF16)
        d = _dot(sel, part)
        acc = d if acc is None else acc + d
        rem = rem - part.astype(F32)
    return acc


def _sigmoid(x):
    return jax.nn.sigmoid(x)


def _silu(x):
    return x * jax.nn.sigmoid(x)


def _proj_kernel(x_ref, g_ref, wm_ref, we_ref, hm_ref, he_ref, xn_ref):
    @pl.when(pl.program_id(1) == 0)
    def _():
        x = x_ref[...]
        ms = jnp.mean(x * x, axis=-1, keepdims=True)
        xn = (x * lax.rsqrt(ms + EPS) * g_ref[...]).astype(BF16)
        xn_ref[...] = xn
        he_ref[...] = _dot(xn, we_ref[...])

    hm_ref[...] = _dot(xn_ref[...], wm_ref[...]).astype(hm_ref.dtype)


def _proj_in(x2, g, wm, we, tm=2048, tn=1024):
    m = x2.shape[0]
    tm = min(tm, m)
    return pl.pallas_call(
        _proj_kernel,
        grid=(m // tm, N_MAIN // tn),
        in_specs=[
            pl.BlockSpec((tm, D_MODEL), lambda i, j: (i, 0)),
            pl.BlockSpec((1, D_MODEL), lambda i, j: (0, 0)),
            pl.BlockSpec((D_MODEL, tn), lambda i, j: (0, j)),
            pl.BlockSpec((D_MODEL, N_EXT), lambda i, j: (0, 0)),
        ],
        out_specs=[
            pl.BlockSpec((tm, tn), lambda i, j: (i, j)),
            pl.BlockSpec((tm, N_EXT), lambda i, j: (i, 0)),
        ],
        out_shape=[
            jax.ShapeDtypeStruct((m, N_MAIN), BF16),
            jax.ShapeDtypeStruct((m, N_EXT), F32),
        ],
        scratch_shapes=[pltpu.VMEM((tm, D_MODEL), BF16)],
        compiler_params=pltpu.CompilerParams(
            dimension_semantics=("parallel", "arbitrary"), vmem_limit_bytes=VMEM_LIMIT),
        name="proj_in",
    )(x2, g, wm, we)


def _rwkv_kernel(has_vres, *refs):
    if has_vres:
        (r_ref, k_ref, v_ref, z_ref, wa_ref, vl_ref, vf_ref, pv_ref, w2_ref, vw2_ref,
         o_ref, state_ref, cr_ref, ck_ref, cv_ref, cwa_ref) = refs
    else:
        (r_ref, k_ref, v_ref, z_ref, wa_ref, pv_ref, w2_ref,
         o_ref, vfo_ref, state_ref, cr_ref, ck_ref, cv_ref, cwa_ref) = refs

    @pl.when(pl.program_id(1) == 0)
    def _():
        state_ref[...] = jnp.zeros_like(state_ref)
        cr_ref[...] = jnp.zeros_like(cr_ref)
        ck_ref[...] = jnp.zeros_like(ck_ref)
        cv_ref[...] = jnp.zeros_like(cv_ref)
        cwa_ref[...] = jnp.zeros_like(cwa_ref)

    row = lax.broadcasted_iota(jnp.int32, (RWKV_TBLK, 1), 0)

    def shift_mix(x_ref, c_ref, mu):
        x = x_ref[...].astype(F32)
        prev = c_ref[7:8, :]
        xs = jnp.where(row == 0, prev, pltpu.roll(x, 1, 0))
        c_ref[...] = x[RWKV_TBLK - 8:RWKV_TBLK, :]
        return x + mu * (xs - x)

    pv = pv_ref[...]
    mu_r, mu_k, mu_v = pv[0:1], pv[1:2], pv[2:3]
    w0, a0, k_k, k_a, r_k = pv[3:4], pv[4:5], pv[5:6], pv[6:7], pv[7:8]
    lnx_g, lnx_b, v0 = pv[8:9], pv[9:10], pv[10:11]
    mu_wa = pv[11:12, 0:LANES]

    r = shift_mix(r_ref, cr_ref, mu_r)
    k = shift_mix(k_ref, ck_ref, mu_k)
    v = shift_mix(v_ref, cv_ref, mu_v)
    wa = shift_mix(wa_ref, cwa_ref, mu_wa)

    lane = lax.broadcasted_iota(jnp.int32, (1, LANES), 1)
    tl = jnp.where(lane < R_LORA, jnp.tanh(wa), wa).astype(BF16)
    pre = _dot(tl, w2_ref[...])
    lw = -math.exp(-0.5) * _sigmoid(w0 + pre[:, :MIX_W])
    a = _sigmoid(a0 + pre[:, MIX_W:])

    if has_vres:
        gate = _sigmoid(v0 + _dot(vl_ref[...].astype(BF16), vw2_ref[...]))
        v = v + (vf_ref[...] - v) * gate
    else:
        vfo_ref[...] = v

    ti = lax.broadcasted_iota(jnp.int32, (RWKV_TBLK, RWKV_TBLK), 0)
    si = lax.broadcasted_iota(jnp.int32, (RWKV_TBLK, RWKV_TBLK), 1)
    same_chunk = jnp.right_shift(ti, 6) == jnp.right_shift(si, 6)
    tri = jnp.where(same_chunk & (si <= ti), 1.0, 0.0).astype(BF16)
    cum = _split_dot(tri, lw, 3)
    ecum = jnp.exp(cum)
    ecum_prev = jnp.exp(cum - lw)
    einv = jnp.exp(-cum)
    chunks = range(RWKV_TBLK // CHUNK)
    rcs = [slice(c * CHUNK, (c + 1) * CHUNK) for c in chunks]
    cls = [cum[(c + 1) * CHUNK - 1:(c + 1) * CHUNK] for c in chunks]
    erests = [jnp.exp(cls[c] - cum[rcs[c]]) for c in chunks]
    g_cs = [jnp.exp(cl) for cl in cls]

    groups = range(N_GROUPS)
    gls = [slice(g * GROUP_W, (g + 1) * GROUP_W) for g in groups]
    units = [(c, g) for c in chunks for g in groups]
    t_i = lax.broadcasted_iota(jnp.int32, (CHUNK, GROUP_W), 0)
    l_i = lax.broadcasted_iota(jnp.int32, (CHUNK, GROUP_W), 1)
    s_i = jnp.bitwise_and(l_i, RWKV_HEAD_W - 1)
    strict = t_i > s_i
    incl = t_i >= s_i
    eye = jnp.where(t_i == s_i, 1.0, 0.0).astype(F32)
    head_of_lane = jnp.right_shift(l_i, 6)
    head_masks = [jnp.where(head_of_lane == h, 1.0, 0.0).astype(BF16)
                  for h in range(GROUP_W // RWKV_HEAD_W)]
    bi = lax.broadcasted_iota(jnp.int32, (GROUP_W, GROUP_W), 0)
    bj = lax.broadcasted_iota(jnp.int32, (GROUP_W, GROUP_W), 1)
    same_head = jnp.right_shift(bi, 6) == jnp.right_shift(bj, 6)
    seg = jnp.where(same_head, 1.0, 0.0).astype(BF16)

    def stack4(x):
        xb = x.astype(BF16)
        return jnp.concatenate([xb * m for m in head_masks], axis=0)

    def segsum(x):
        return _dot(x.astype(BF16), seg)

    kk_raw = k * k_k
    k2 = k * (1.0 + (a - 1.0) * k_a)
    sums2 = [segsum(jnp.concatenate([kk_raw[:, gl] * kk_raw[:, gl],
                                     r[:, gl] * k2[:, gl] * r_k[:, gl]], axis=0)) for gl in gls]
    lhss, rhss, bks, v_sts = {}, {}, {}, {}
    for c, g in units:
        rc, gl = rcs[c], gls[g]
        kk = kk_raw[rc, gl] * lax.rsqrt(jnp.maximum(sums2[g][rc], 1e-24))
        kka = kk * a[rc, gl]
        at = -kk * ecum_prev[rc, gl]
        bt = kka * einv[rc, gl]
        kt = k2[rc, gl] * einv[rc, gl]
        rt = r[rc, gl] * ecum[rc, gl]
        bh = kka * erests[c][:, gl]
        kh = k2[rc, gl] * erests[c][:, gl]
        lhss[c, g] = jnp.concatenate([at, rt], axis=0).astype(BF16)
        rhss[c, g] = jnp.concatenate([stack4(bt), stack4(kt)], axis=0)
        bks[c, g] = jnp.concatenate([bh, kh], axis=0).astype(BF16)
        v_sts[c, g] = stack4(v[rc, gl])

    gs = {u: _dot_nt(lhss[u], rhss[u]) for u in units}
    n_abs = {u: jnp.where(strict, gs[u][0:CHUNK, 0:GROUP_W], 0.0) for u in units}
    g_ks = {u: jnp.concatenate([jnp.where(strict, gs[u][0:CHUNK, GROUP_W:], 0.0),
                                jnp.where(incl, gs[u][CHUNK:, GROUP_W:], 0.0)],
                               axis=0).astype(BF16) for u in units}
    g_rbs = {u: jnp.where(incl, gs[u][CHUNK:, 0:GROUP_W], 0.0).astype(BF16) for u in units}
    gkvs = {u: _dot(g_ks[u], v_sts[u]) for u in units}

    psts = {u: stack4(n_abs[u]) for u in units}
    pws = {u: _dot(n_abs[u].astype(BF16), psts[u]) for u in units}
    sks = {u: eye + n_abs[u] for u in units}
    for _ in range(4):
        psts = {u: stack4(pws[u]) for u in units}
        prods = {u: _dot(jnp.concatenate([sks[u], pws[u]], axis=0).astype(BF16), psts[u])
                 for u in units}
        sks = {u: sks[u] + prods[u][0:CHUNK] for u in units}
        pws = {u: prods[u][CHUNK:] for u in units}
    psts = {u: stack4(pws[u]) for u in units}
    tinvs = {u: (sks[u] + _dot(sks[u].astype(BF16), psts[u])).astype(BF16) for u in units}

    states = [state_ref[g] for g in groups]
    ys = {}
    for c in chunks:
        ahs = [_dot_nt(lhss[c, g], states[g].astype(BF16)) for g in groups]
        us = [_dot(tinvs[c, g], stack4(ahs[g][0:CHUNK] + gkvs[c, g][0:CHUNK])) for g in groups]
        for g in groups:
            ys[c, g] = (ahs[g][CHUNK:] + gkvs[c, g][CHUNK:]
                        + _dot(g_rbs[c, g], stack4(us[g])))
        uvs = [jnp.concatenate([us[g], v[rcs[c], gls[g]]], axis=0).astype(BF16) for g in groups]
        states = [states[g] * g_cs[c][:, gls[g]]
                  + jnp.where(same_head, _dot_tn(uvs[g], bks[c, g]), 0.0) for g in groups]
    state_ref[...] = jnp.stack(states, axis=0)

    y_rows = [jnp.concatenate([ys[c, g] for c in chunks], axis=0) for g in groups]
    means = [segsum(y) * (1.0 / RWKV_HEAD_W) for y in y_rows]
    ds = [y_rows[g] - means[g] for g in groups]
    variances = [segsum(d * d) * (1.0 / RWKV_HEAD_W) for d in ds]
    outs = []
    for g, gl in enumerate(gls):
        yn = ds[g] * lax.rsqrt(variances[g] + GN_EPS) * lnx_g[:, gl] + lnx_b[:, gl]
        bonus = sums2[g][RWKV_TBLK:] * v[:, gl]
        outs.append((yn + bonus) * _silu(z_ref[:, gl].astype(F32)))
    o_ref[...] = jnp.concatenate(outs, axis=1).astype(o_ref.dtype)


def _rwkv(hm, he, vfirst, pvec, w2, vw2, batch, seq):
    nt = seq // RWKV_TBLK
    has_vres = vfirst is not None
    tok = lambda b, t: b * nt + t
    slab = lambda s: pl.BlockSpec((RWKV_TBLK, MIX_W), lambda b, t: (tok(b, t), s))
    ext = lambda s: pl.BlockSpec((RWKV_TBLK, LANES), lambda b, t: (tok(b, t), s))
    full = lambda a: pl.BlockSpec(a.shape, lambda b, t: (0,) * a.ndim)
    act = pl.BlockSpec((RWKV_TBLK, MIX_W), lambda b, t: (tok(b, t), 0))
    in_specs = [slab(SLAB_R), slab(SLAB_K), slab(SLAB_V), slab(SLAB_ZA), ext(0)]
    args = [hm, hm, hm, hm, he]
    if has_vres:
        in_specs += [ext(1), act, full(pvec), full(w2), full(vw2)]
        args += [he, vfirst, pvec, w2, vw2]
        out_specs = act
        out_shape = jax.ShapeDtypeStruct((batch * seq, MIX_W), BF16)
    else:
        in_specs += [full(pvec), full(w2)]
        args += [pvec, w2]
        out_specs = [act, act]
        out_shape = [jax.ShapeDtypeStruct((batch * seq, MIX_W), BF16),
                     jax.ShapeDtypeStruct((batch * seq, MIX_W), F32)]
    return pl.pallas_call(
        functools.partial(_rwkv_kernel, has_vres),
        grid=(batch, nt),
        in_specs=in_specs,
        out_specs=out_specs,
        out_shape=out_shape,
        scratch_shapes=[
            pltpu.VMEM((N_GROUPS, GROUP_W, GROUP_W), F32),
            pltpu.VMEM((8, MIX_W), F32),
            pltpu.VMEM((8, MIX_W), F32),
            pltpu.VMEM((8, MIX_W), F32),
            pltpu.VMEM((8, LANES), F32),
        ],
        compiler_params=pltpu.CompilerParams(
            dimension_semantics=("parallel", "arbitrary"), vmem_limit_bytes=VMEM_LIMIT),
        name="rwkv",
    )(*args)


def _sgu_kernel(u_ref, v_ref, z_ref, lng_ref, lnb_ref, ws_ref, bias_ref, o_ref):
    v = v_ref[...].astype(F32)
    mean = jnp.mean(v, axis=-1, keepdims=True)
    d = v - mean
    var = jnp.mean(d * d, axis=-1, keepdims=True)
    vn = (d * lax.rsqrt(var + EPS) * lng_ref[...] + lnb_ref[...]).astype(BF16)
    ti = lax.broadcasted_iota(jnp.int32, (SGU_BLOCK, SGU_BLOCK), 0)
    si = lax.broadcasted_iota(jnp.int32, (SGU_BLOCK, SGU_BLOCK), 1)
    causal = (ti >= CHUNK) | (si < CHUNK)
    for g in range(SGU_GROUPS):
        sl = slice(g * LANES, (g + 1) * LANES)
        ws = jnp.where(causal, ws_ref[g], 0.0).astype(BF16)
        mixed = _dot(ws, vn[:, sl]) + bias_ref[:, sl]
        o_ref[:, sl] = (u_ref[:, sl].astype(F32) * mixed
                        * _silu(z_ref[:, sl].astype(F32))).astype(o_ref.dtype)


def _sgu(hm, ln_g, ln_b, ws, bias, batch, seq):
    nb = seq // SGU_BLOCK
    slab = lambda s: pl.BlockSpec((SGU_BLOCK, MIX_W), lambda b, t: (b * nb + t, s))
    full = lambda a: pl.BlockSpec(a.shape, lambda b, t: (0,) * a.ndim)
    return pl.pallas_call(
        _sgu_kernel,
        grid=(batch, nb),
        in_specs=[slab(SLAB_UB), slab(SLAB_VB), slab(SLAB_ZB),
                  full(ln_g), full(ln_b), full(ws), full(bias)],
        out_specs=pl.BlockSpec((SGU_BLOCK, MIX_W), lambda b, t: (b * nb + t, 0)),
        out_shape=jax.ShapeDtypeStruct((batch * seq, MIX_W), BF16),
        compiler_params=pltpu.CompilerParams(
            dimension_semantics=("parallel", "parallel"), vmem_limit_bytes=VMEM_LIMIT),
        name="sgu",
    )(hm, hm, hm, ln_g, ln_b, ws, bias)


def _hgrn_kernel(q_ref, f_ref, i_ref, z_ref, lb_ref, g_ref, o_ref, state_ref, kx_ref, fg_ref,
                 ic_ref):
    @pl.when(pl.program_id(2) == 0)
    def _():
        state_ref[...] = jnp.zeros_like(state_ref)

    lb = lb_ref[...]
    gi = lax.broadcasted_iota(jnp.int32, (HGRN_TBLK, HGRN_TBLK), 0)
    gj = lax.broadcasted_iota(jnp.int32, (HGRN_TBLK, HGRN_TBLK), 1)
    same_step = jnp.right_shift(gi, 4) == jnp.right_shift(gj, 4)
    tri = jnp.where(same_step & (gj <= gi), 1.0, 0.0).astype(BF16)
    ones = jnp.ones((HGRN_HEAD, HGRN_HEAD), BF16)
    half = HGRN_STEP // 2
    trow = lax.broadcasted_iota(jnp.int32, (half, 1), 0)

    q = _silu(q_ref[...].astype(F32))
    fg = lb + (1.0 - lb) * _sigmoid(f_ref[...].astype(F32))
    kx = 1.0 - fg
    ic = i_ref[...].astype(F32)
    kx_ref[...] = kx
    fg_ref[...] = fg
    ic_ref[...] = ic

    def row_bcast(ref, r):
        return jnp.broadcast_to(ref[r:r + 1, :], (half, HGRN_HEAD))

    bcs = _split_dot(tri, jnp.log(fg), 3)
    steps = range(HGRN_TBLK // HGRN_STEP)
    rss = [slice(j * HGRN_STEP, (j + 1) * HGRN_STEP) for j in steps]
    bls = [bcs[rs][HGRN_STEP - 1:HGRN_STEP] for rs in rss]
    atts_lo, atts_hi = [], []
    for j, rs in enumerate(rss):
        qs, base = q[rs], j * HGRN_STEP
        q_lo, q_hi = qs[0:half], qs[half:HGRN_STEP]
        p_lo, p_hi = [None] * half, [None] * HGRN_STEP
        lo = hi = None
        for s in range(HGRN_STEP - 1, -1, -1):
            nxt = row_bcast(fg_ref, base + s + 1) if s + 1 < HGRN_STEP else None
            k_s = row_bcast(kx_ref, base + s)
            if s >= half:
                prev = 0.0 if hi is None else hi * nxt
                hi = jnp.where(trow == s - half, q_hi, prev)
            else:
                hi = hi * nxt
                prev = 0.0 if lo is None else lo * nxt
                lo = jnp.where(trow == s, q_lo, prev)
                p_lo[s] = lo * k_s
            p_hi[s] = hi * k_s
        stack = jnp.concatenate(p_lo + p_hi, axis=0).astype(BF16)
        att = _dot(stack, ones)
        atts_lo.append(att[0:half * half])
        atts_hi.append(att[half * half:])
    incs = [_dot_tn(ic[rs].astype(BF16), (kx[rs] * jnp.exp(bl - bcs[rs])).astype(BF16))
            for rs, bl in zip(rss, bls)]
    qb = (q * jnp.exp(bcs)).astype(BF16)
    st = state_ref[...]
    o_inters = []
    for j in steps:
        o_inters.append(_dot_nt(qb[rss[j]], st.astype(BF16)))
        st = st * jnp.exp(bls[j]) + incs[j]
    state_ref[...] = st
    outs = []
    for j, rs in enumerate(rss):
        a_lo, a_hi, base = atts_lo[j], atts_hi[j], j * HGRN_STEP
        o_lo = a_lo[0:half] * row_bcast(ic_ref, base)
        for s in range(1, half):
            o_lo = o_lo + a_lo[s * half:(s + 1) * half] * row_bcast(ic_ref, base + s)
        o_hi = a_hi[0:half] * row_bcast(ic_ref, base)
        for s in range(1, HGRN_STEP):
            o_hi = o_hi + a_hi[s * half:(s + 1) * half] * row_bcast(ic_ref, base + s)
        outs.append(o_inters[j] + jnp.concatenate([o_lo, o_hi], axis=0))
    o = jnp.concatenate(outs, axis=0)
    o = o * lax.rsqrt(jnp.mean(o * o, axis=-1, keepdims=True) + EPS) * g_ref[...]
    o_ref[...] = (o * _silu(z_ref[...].astype(F32))).astype(o_ref.dtype)


def _hgrn(hm, lb, g_norm, batch, seq):
    nt = seq // HGRN_TBLK
    col = lambda s: pl.BlockSpec(
        (HGRN_TBLK, HGRN_HEAD), lambda b, h, t: (b * nt + t, s * HGRN_HEADS + h))
    vec = pl.BlockSpec((1, HGRN_HEAD), lambda b, h, t: (0, h))
    return pl.pallas_call(
        _hgrn_kernel,
        grid=(batch, HGRN_HEADS, nt),
        in_specs=[col(SLAB_QC), col(SLAB_FC), col(SLAB_IC), col(SLAB_ZC), vec, vec],
        out_specs=pl.BlockSpec((HGRN_TBLK, HGRN_HEAD), lambda b, h, t: (b * nt + t, h)),
        out_shape=jax.ShapeDtypeStruct((batch * seq, MIX_W), BF16),
        scratch_shapes=[pltpu.VMEM((HGRN_HEAD, HGRN_HEAD), F32),
                        pltpu.VMEM((HGRN_TBLK, HGRN_HEAD), F32),
                        pltpu.VMEM((HGRN_TBLK, HGRN_HEAD), F32),
                        pltpu.VMEM((HGRN_TBLK, HGRN_HEAD), F32)],
        compiler_params=pltpu.CompilerParams(
            dimension_semantics=("parallel", "parallel", "arbitrary"),
            vmem_limit_bytes=VMEM_LIMIT),
        name="hgrn",
    )(hm, hm, hm, hm, lb, g_norm)


def _merge_kernel(final, *refs):
    if final:
        oa, ob, oc, g0, g1, g2, x_ref, bm_ref, wb_ref, wo_ref, fg_ref, out_ref = refs
    else:
        oa, ob, oc, g0, g1, g2, x_ref, bm_ref, wb_ref, wo_ref, out_ref = refs
    y = None
    for n, (o_ref, gl_ref) in enumerate(((oa, g0), (ob, g1), (oc, g2))):
        proj = _dot(o_ref[...], wb_ref[n])
        term = _sigmoid(gl_ref[...].astype(F32) + bm_ref[n:n + 1, :]) * proj
        y = term if y is None else y + term
    xo = x_ref[...] + _dot(y.astype(BF16), wo_ref[...])
    if final:
        ms = jnp.mean(xo * xo, axis=-1, keepdims=True)
        xo = xo * lax.rsqrt(ms + EPS) * fg_ref[...]
    out_ref[...] = xo


def _merge(oa, ob, oc, hm, x2, bm, wb, wo, final_g, tm=256):
    m = x2.shape[0]
    tm = min(tm, m)
    final = final_g is not None
    act = pl.BlockSpec((tm, D_MODEL), lambda i: (i, 0))
    gl = lambda n: pl.BlockSpec((tm, D_MODEL), lambda i: (i, SLAB_GL + n))
    full = lambda a: pl.BlockSpec(a.shape, lambda i: (0,) * a.ndim)
    in_specs = [act, act, act, gl(0), gl(1), gl(2), act, full(bm), full(wb), full(wo)]
    args = [oa, ob, oc, hm, hm, hm, x2, bm, wb, wo]
    if final:
        in_specs.append(full(final_g))
        args.append(final_g)
    return pl.pallas_call(
        functools.partial(_merge_kernel, final),
        grid=(m // tm,),
        in_specs=in_specs,
        out_specs=act,
        out_shape=jax.ShapeDtypeStruct((m, D_MODEL), F32),
        compiler_params=pltpu.CompilerParams(
            dimension_semantics=("parallel",), vmem_limit_bytes=VMEM_LIMIT),
        name="merge",
    )(*args)


def kernel(x, norm_g, w_in, b_merge, mu_shift, w0, w_w2, a0, a_w2, v0, v_w1, v_w2, k_k, k_a, r_k,
           lnx_g, lnx_b, sgu_ln_g, sgu_ln_b, sgu_w, sgu_b, lb_logits, hgrn_g, w_branch, w_out,
           final_g):
    batch, seq, _ = x.shape
    depth = w_in.shape[0]
    assert seq % HGRN_TBLK == 0 and x.shape[2] == D_MODEL
    n_shift = 3 * MIX_W + 2 * R_LORA

    lb_all = jnp.cumsum(jax.nn.softmax(lb_logits.astype(F32), axis=0), axis=0)
    lb_all = lb_all - lb_all[0]

    x2 = x.reshape(batch * seq, D_MODEL)
    vfirst = None
    for l in range(depth):
        w = w_in[l]
        wm = jnp.concatenate([w[:, :3 * MIX_W], w[:, n_shift:]], axis=1).astype(BF16)
        ext = [w[:, 3 * MIX_W:n_shift]]
        if l > 0:
            ext.append(v_w1[l - 1])
        we = jnp.concatenate(ext, axis=1)
        we = jnp.pad(we, ((0, 0), (0, N_EXT - we.shape[1]))).astype(BF16)

        zrow = jnp.zeros((MIX_W,), F32)
        mu = mu_shift[l]
        rows = [mu[0:MIX_W], mu[MIX_W:2 * MIX_W], mu[2 * MIX_W:3 * MIX_W], w0[l], a0[l], k_k[l],
                k_a[l], r_k[l], lnx_g[l], lnx_b[l],
                v0[l - 1] if l > 0 else zrow,
                jnp.pad(mu[3 * MIX_W:n_shift], (0, MIX_W - 2 * R_LORA))]
        pvec = jnp.stack(rows + [zrow] * (16 - len(rows)), axis=0)
        zl = jnp.zeros((R_LORA, MIX_W), F32)
        w2 = jnp.concatenate(
            [jnp.concatenate([w_w2[l], zl], axis=0), jnp.concatenate([zl, a_w2[l]], axis=0)],
            axis=1).astype(BF16)

        hm, he = _proj_in(x2, norm_g[l][None, :], wm, we)

        if l == 0:
            oa, vfirst = _rwkv(hm, he, None, pvec, w2, None, batch, seq)
        else:
            vw2 = jnp.pad(v_w2[l - 1], ((0, LANES - R_V), (0, 0))).astype(BF16)
            oa = _rwkv(hm, he, vfirst, pvec, w2, vw2, batch, seq)

        bias = jnp.repeat(sgu_b[l].T, LANES, axis=1)
        ob = _sgu(hm, sgu_ln_g[l][None, :], sgu_ln_b[l][None, :], sgu_w[l], bias, batch, seq)
        oc = _hgrn(hm, lb_all[l][None, :], hgrn_g[l][None, :], batch, seq)

        bm = jnp.pad(b_merge[l], ((0, 8 - N_BRANCH), (0, 0)))
        x2 = _merge(oa, ob, oc, hm, x2, bm, w_branch[l].astype(BF16), w_out[l].astype(BF16),
                    final_g[None, :] if l == depth - 1 else None)
    return x2.reshape(batch, seq, D_MODEL)
```

```python
import functools
import math

import jax
import jax.numpy as jnp
from jax import lax
from jax.experimental import pallas as pl
from jax.experimental.pallas import tpu as pltpu

F32 = jnp.float32
BF16 = jnp.bfloat16

D_MODEL = 1024
MIX_W = 1024
N_BRANCH = 3
CHUNK = 64
RWKV_TBLK = 256
RWKV_HEAD_W = 64
R_LORA = 64
R_V = 32
SGU_BLOCK = 128
SGU_TBLK = 512
SGU_GROUPS = 8
HGRN_HEAD = 128
HGRN_HEADS = MIX_W // HGRN_HEAD
HGRN_STEP = 16
HGRN_SUB = 512
HGRN_TBLK = 1024
EPS = 1e-6
GN_EPS = 64e-5
LANES = 128
GROUP_W = 256
N_GROUPS = MIX_W // GROUP_W
N_MAIN = 14 * 1024
N_EXT = 256
VMEM_LIMIT = 48 * 1024 * 1024

SLAB_R, SLAB_K, SLAB_V, SLAB_ZA, SLAB_UB, SLAB_VB, SLAB_ZB = 0, 1, 2, 3, 4, 5, 6
SLAB_QC, SLAB_FC, SLAB_IC, SLAB_ZC, SLAB_GL = 7, 8, 9, 10, 11


def _dot(a, b):
    return jnp.dot(a, b, preferred_element_type=F32)


def _dot_nt(a, b):
    return lax.dot_general(a, b, (((1,), (1,)), ((), ())), preferred_element_type=F32)


def _dot_tn(a, b):
    return lax.dot_general(a, b, (((0,), (0,)), ((), ())), preferred_element_type=F32)


def _split_dot(sel, x, terms):
    acc = None
    rem = x
    for _ in range(terms):
        part = rem.astype(BF16)
        d = _dot(sel, part)
        acc = d if acc is None else acc + d
        rem = rem - part.astype(F32)
    return acc


def _sigmoid(x):
    return jax.nn.sigmoid(x)


def _silu(x):
    return x * jax.nn.sigmoid(x)


def _proj_kernel(x_ref, g_ref, wm_ref, we_ref, hm_ref, he_ref, xn_ref):
    @pl.when(pl.program_id(1) == 0)
    def _():
        x = x_ref[...]
        ms = jnp.mean(x * x, axis=-1, keepdims=True)
        xn = (x * lax.rsqrt(ms + EPS) * g_ref[...]).astype(BF16)
        xn_ref[...] = xn
        he_ref[...] = _dot(xn, we_ref[...])

    hm_ref[...] = _dot(xn_ref[...], wm_ref[...]).astype(hm_ref.dtype)


def _proj_in(x2, g, wm, we, tm=2048, tn=1024):
    m = x2.shape[0]
    tm = min(tm, m)
    return pl.pallas_call(
        _proj_kernel,
        grid=(m // tm, N_MAIN // tn),
        in_specs=[
            pl.BlockSpec((tm, D_MODEL), lambda i, j: (i, 0)),
            pl.BlockSpec((1, D_MODEL), lambda i, j: (0, 0)),
            pl.BlockSpec((D_MODEL, tn), lambda i, j: (0, j)),
            pl.BlockSpec((D_MODEL, N_EXT), lambda i, j: (0, 0)),
        ],
        out_specs=[
            pl.BlockSpec((tm, tn), lambda i, j: (i, j)),
            pl.BlockSpec((tm, N_EXT), lambda i, j: (i, 0)),
        ],
        out_shape=[
            jax.ShapeDtypeStruct((m, N_MAIN), BF16),
            jax.ShapeDtypeStruct((m, N_EXT), F32),
        ],
        scratch_shapes=[pltpu.VMEM((tm, D_MODEL), BF16)],
        compiler_params=pltpu.CompilerParams(
            dimension_semantics=("parallel", "arbitrary"), vmem_limit_bytes=VMEM_LIMIT),
        name="proj_in",
    )(x2, g, wm, we)


def _rwkv_kernel(has_vres, *refs):
    if has_vres:
        (r_ref, k_ref, v_ref, z_ref, wa_ref, vl_ref, vf_ref, pv_ref, w2_ref, vw2_ref,
         o_ref, state_ref, cr_ref, ck_ref, cv_ref, cwa_ref) = refs
    else:
        (r_ref, k_ref, v_ref, z_ref, wa_ref, pv_ref, w2_ref,
         o_ref, vfo_ref, state_ref, cr_ref, ck_ref, cv_ref, cwa_ref) = refs

    @pl.when(pl.program_id(1) == 0)
    def _():
        state_ref[...] = jnp.zeros_like(state_ref)
        cr_ref[...] = jnp.zeros_like(cr_ref)
        ck_ref[...] = jnp.zeros_like(ck_ref)
        cv_ref[...] = jnp.zeros_like(cv_ref)
        cwa_ref[...] = jnp.zeros_like(cwa_ref)

    row = lax.broadcasted_iota(jnp.int32, (RWKV_TBLK, 1), 0)

    def shift_mix(x_ref, c_ref, mu):
        x = x_ref[...].astype(F32)
        prev = c_ref[7:8, :]
        xs = jnp.where(row == 0, prev, pltpu.roll(x, 1, 0))
        c_ref[...] = x[RWKV_TBLK - 8:RWKV_TBLK, :]
        return x + mu * (xs - x)

    pv = pv_ref[...]
    mu_r, mu_k, mu_v = pv[0:1], pv[1:2], pv[2:3]
    w0, a0, k_k, k_a, r_k = pv[3:4], pv[4:5], pv[5:6], pv[6:7], pv[7:8]
    lnx_g, lnx_b, v0 = pv[8:9], pv[9:10], pv[10:11]
    mu_wa = pv[11:12, 0:LANES]

    r = shift_mix(r_ref, cr_ref, mu_r)
    k = shift_mix(k_ref, ck_ref, mu_k)
    v = shift_mix(v_ref, cv_ref, mu_v)
    wa = shift_mix(wa_ref, cwa_ref, mu_wa)

    lane = lax.broadcasted_iota(jnp.int32, (1, LANES), 1)
    tl = jnp.where(lane < R_LORA, jnp.tanh(wa), wa).astype(BF16)
    pre = _dot(tl, w2_ref[...])
    lw = -math.exp(-0.5) * _sigmoid(w0 + pre[:, :MIX_W])
    a = _sigmoid(a0 + pre[:, MIX_W:])

    if has_vres:
        gate = _sigmoid(v0 + _dot(vl_ref[...].astype(BF16), vw2_ref[...]))
        v = v + (vf_ref[...] - v) * gate
    else:
        vfo_ref[...] = v

    ti = lax.broadcasted_iota(jnp.int32, (RWKV_TBLK, RWKV_TBLK), 0)
    si = lax.broadcasted_iota(jnp.int32, (RWKV_TBLK, RWKV_TBLK), 1)
    same_chunk = jnp.right_shift(ti, 6) == jnp.right_shift(si, 6)
    tri = jnp.where(same_chunk & (si <= ti), 1.0, 0.0).astype(BF16)
    cum = _split_dot(tri, lw, 3)
    ecum = jnp.exp(cum)
    ecum_prev = jnp.exp(cum - lw)
    einv = jnp.exp(-cum)
    chunks = range(RWKV_TBLK // CHUNK)
    rcs = [slice(c * CHUNK, (c + 1) * CHUNK) for c in chunks]
    cls = [cum[(c + 1) * CHUNK - 1:(c + 1) * CHUNK] for c in chunks]
    g_cs = [jnp.exp(cl) for cl in cls]

    groups = range(N_GROUPS)
    gls = [slice(g * GROUP_W, (g + 1) * GROUP_W) for g in groups]
    units = [(c, g) for c in chunks for g in groups]
    t_i = lax.broadcasted_iota(jnp.int32, (CHUNK, GROUP_W), 0)
    l_i = lax.broadcasted_iota(jnp.int32, (CHUNK, GROUP_W), 1)
    s_i = jnp.bitwise_and(l_i, RWKV_HEAD_W - 1)
    strict = t_i > s_i
    incl = t_i >= s_i
    eye = jnp.where(t_i == s_i, 1.0, 0.0).astype(F32)
    head_of_lane = jnp.right_shift(l_i, 6)
    head_masks = [jnp.where(head_of_lane == h, 1.0, 0.0).astype(BF16)
                  for h in range(GROUP_W // RWKV_HEAD_W)]
    bi = lax.broadcasted_iota(jnp.int32, (GROUP_W, GROUP_W), 0)
    bj = lax.broadcasted_iota(jnp.int32, (GROUP_W, GROUP_W), 1)
    same_head = jnp.right_shift(bi, 6) == jnp.right_shift(bj, 6)
    seg = jnp.where(same_head, 1.0, 0.0).astype(BF16)

    def stack4(x):
        xb = x.astype(BF16)
        return jnp.concatenate([xb * m for m in head_masks], axis=0)

    def segsum(x):
        return _dot(x.astype(BF16), seg)

    kk_raw = k * k_k
    k2 = k * (1.0 + (a - 1.0) * k_a)
    sums2 = [segsum(jnp.concatenate([kk_raw[:, gl] * kk_raw[:, gl],
                                     r[:, gl] * k2[:, gl] * r_k[:, gl]], axis=0)) for gl in gls]
    lhss, rhss, bks, v_sts = {}, {}, {}, {}
    for c, g in units:
        rc, gl = rcs[c], gls[g]
        kk = kk_raw[rc, gl] * lax.rsqrt(jnp.maximum(sums2[g][rc], 1e-24))
        kka = kk * a[rc, gl]
        at = -kk * ecum_prev[rc, gl]
        bt = kka * einv[rc, gl]
        kt = k2[rc, gl] * einv[rc, gl]
        rt = r[rc, gl] * ecum[rc, gl]
        lhss[c, g] = jnp.concatenate([at, rt], axis=0).astype(BF16)
        rhss[c, g] = jnp.concatenate([stack4(bt), stack4(kt)], axis=0)
        bks[c, g] = jnp.concatenate([bt, kt], axis=0).astype(BF16)
        v_sts[c, g] = stack4(v[rc, gl])

    gs = {u: _dot_nt(lhss[u], rhss[u]) for u in units}
    n_abs = {u: jnp.where(strict, gs[u][0:CHUNK, 0:GROUP_W], 0.0) for u in units}
    g_ks = {u: jnp.concatenate([jnp.where(strict, gs[u][0:CHUNK, GROUP_W:], 0.0),
                                jnp.where(incl, gs[u][CHUNK:, GROUP_W:], 0.0)],
                               axis=0).astype(BF16) for u in units}
    g_rbs = {u: jnp.where(incl, gs[u][CHUNK:, 0:GROUP_W], 0.0).astype(BF16) for u in units}
    gkvs = {u: _dot(g_ks[u], v_sts[u]) for u in units}

    psts = {u: stack4(n_abs[u]) for u in units}
    pws = {u: _dot(n_abs[u].astype(BF16), psts[u]) for u in units}
    sks = {u: eye + n_abs[u] for u in units}
    for _ in range(4):
        psts = {u: stack4(pws[u]) for u in units}
        prods = {u: _dot(jnp.concatenate([sks[u], pws[u]], axis=0).astype(BF16), psts[u])
                 for u in units}
        sks = {u: sks[u] + prods[u][0:CHUNK] for u in units}
        pws = {u: prods[u][CHUNK:] for u in units}
    psts = {u: stack4(pws[u]) for u in units}
    tinvs = {u: (sks[u] + _dot(sks[u].astype(BF16), psts[u])).astype(BF16) for u in units}

    states = [state_ref[g] for g in groups]
    ys = {}
    for c in chunks:
        ahs = [_dot_nt(lhss[c, g], states[g].astype(BF16)) for g in groups]
        us = [_dot(tinvs[c, g], stack4(ahs[g][0:CHUNK] + gkvs[c, g][0:CHUNK])) for g in groups]
        for g in groups:
            ys[c, g] = (ahs[g][CHUNK:] + gkvs[c, g][CHUNK:]
                        + _dot(g_rbs[c, g], stack4(us[g])))
        uvs = [jnp.concatenate([us[g], v[rcs[c], gls[g]]], axis=0).astype(BF16) for g in groups]
        states = [(states[g] + jnp.where(same_head, _dot_tn(uvs[g], bks[c, g]), 0.0))
                  * g_cs[c][:, gls[g]] for g in groups]
    state_ref[...] = jnp.stack(states, axis=0)

    y_rows = [jnp.concatenate([ys[c, g] for c in chunks], axis=0) for g in groups]
    means = [segsum(y) * (1.0 / RWKV_HEAD_W) for y in y_rows]
    ds = [y_rows[g] - means[g] for g in groups]
    variances = [segsum(d * d) * (1.0 / RWKV_HEAD_W) for d in ds]
    outs = []
    for g, gl in enumerate(gls):
        yn = ds[g] * lax.rsqrt(variances[g] + GN_EPS) * lnx_g[:, gl] + lnx_b[:, gl]
        bonus = sums2[g][RWKV_TBLK:] * v[:, gl]
        outs.append((yn + bonus) * _silu(z_ref[:, gl].astype(F32)))
    o_ref[...] = jnp.concatenate(outs, axis=1).astype(o_ref.dtype)


def _rwkv(hm, he, vfirst, pvec, w2, vw2, batch, seq):
    nt = seq // RWKV_TBLK
    has_vres = vfirst is not None
    tok = lambda b, t: b * nt + t
    slab = lambda s: pl.BlockSpec((RWKV_TBLK, MIX_W), lambda b, t: (tok(b, t), s))
    ext = lambda s: pl.BlockSpec((RWKV_TBLK, LANES), lambda b, t: (tok(b, t), s))
    full = lambda a: pl.BlockSpec(a.shape, lambda b, t: (0,) * a.ndim)
    act = pl.BlockSpec((RWKV_TBLK, MIX_W), lambda b, t: (tok(b, t), 0))
    in_specs = [slab(SLAB_R), slab(SLAB_K), slab(SLAB_V), slab(SLAB_ZA), ext(0)]
    args = [hm, hm, hm, hm, he]
    if has_vres:
        in_specs += [ext(1), act, full(pvec), full(w2), full(vw2)]
        args += [he, vfirst, pvec, w2, vw2]
        out_specs = act
        out_shape = jax.ShapeDtypeStruct((batch * seq, MIX_W), BF16)
    else:
        in_specs += [full(pvec), full(w2)]
        args += [pvec, w2]
        out_specs = [act, act]
        out_shape = [jax.ShapeDtypeStruct((batch * seq, MIX_W), BF16),
                     jax.ShapeDtypeStruct((batch * seq, MIX_W), F32)]
    return pl.pallas_call(
        functools.partial(_rwkv_kernel, has_vres),
        grid=(batch, nt),
        in_specs=in_specs,
        out_specs=out_specs,
        out_shape=out_shape,
        scratch_shapes=[
            pltpu.VMEM((N_GROUPS, GROUP_W, GROUP_W), F32),
            pltpu.VMEM((8, MIX_W), F32),
            pltpu.VMEM((8, MIX_W), F32),
            pltpu.VMEM((8, MIX_W), F32),
            pltpu.VMEM((8, LANES), F32),
        ],
        compiler_params=pltpu.CompilerParams(
            dimension_semantics=("parallel", "arbitrary"), vmem_limit_bytes=VMEM_LIMIT),
        name="rwkv",
    )(*args)


def _sgu_kernel(u_ref, v_ref, z_ref, lng_ref, lnb_ref, ws_ref, bias_ref, o_ref):
    v = v_ref[...].astype(F32)
    mean = jnp.mean(v, axis=-1, keepdims=True)
    d = v - mean
    var = jnp.mean(d * d, axis=-1, keepdims=True)
    vn = (d * lax.rsqrt(var + EPS) * lng_ref[...] + lnb_ref[...]).astype(BF16)
    ti = lax.broadcasted_iota(jnp.int32, (SGU_BLOCK, SGU_BLOCK), 0)
    si = lax.broadcasted_iota(jnp.int32, (SGU_BLOCK, SGU_BLOCK), 1)
    causal = (ti >= CHUNK) | (si < CHUNK)
    for g in range(SGU_GROUPS):
        sl = slice(g * LANES, (g + 1) * LANES)
        ws = jnp.where(causal, ws_ref[g], 0.0).astype(BF16)
        for b in range(SGU_TBLK // SGU_BLOCK):
            rows = slice(b * SGU_BLOCK, (b + 1) * SGU_BLOCK)
            mixed = _dot(ws, vn[rows, sl]) + bias_ref[:, sl]
            o_ref[rows, sl] = (u_ref[rows, sl].astype(F32) * mixed
                               * _silu(z_ref[rows, sl].astype(F32))).astype(o_ref.dtype)


def _sgu(hm, ln_g, ln_b, ws, bias, batch, seq):
    nb = seq // SGU_TBLK
    slab = lambda s: pl.BlockSpec((SGU_TBLK, MIX_W), lambda b, t: (b * nb + t, s))
    full = lambda a: pl.BlockSpec(a.shape, lambda b, t: (0,) * a.ndim)
    return pl.pallas_call(
        _sgu_kernel,
        grid=(batch, nb),
        in_specs=[slab(SLAB_UB), slab(SLAB_VB), slab(SLAB_ZB),
                  full(ln_g), full(ln_b), full(ws), full(bias)],
        out_specs=pl.BlockSpec((SGU_TBLK, MIX_W), lambda b, t: (b * nb + t, 0)),
        out_shape=jax.ShapeDtypeStruct((batch * seq, MIX_W), BF16),
        compiler_params=pltpu.CompilerParams(
            dimension_semantics=("parallel", "parallel"), vmem_limit_bytes=VMEM_LIMIT),
        name="sgu",
    )(hm, hm, hm, ln_g, ln_b, ws, bias)


def _hgrn_kernel(q_ref, f_ref, i_ref, z_ref, lb_ref, g_ref, o_ref, state_ref, kx_ref, fg_ref,
                 ic_ref):
    @pl.when(pl.program_id(2) == 0)
    def _():
        state_ref[...] = jnp.zeros_like(state_ref)

    for sub in range(HGRN_TBLK // HGRN_SUB):
        _hgrn_block(slice(sub * HGRN_SUB, (sub + 1) * HGRN_SUB), q_ref, f_ref, i_ref, z_ref,
                    lb_ref, g_ref, o_ref, state_ref, kx_ref, fg_ref, ic_ref)


def _hgrn_block(rows, q_ref, f_ref, i_ref, z_ref, lb_ref, g_ref, o_ref, state_ref, kx_ref, fg_ref,
                ic_ref):
    lb = lb_ref[...]
    gi = lax.broadcasted_iota(jnp.int32, (HGRN_SUB, HGRN_SUB), 0)
    gj = lax.broadcasted_iota(jnp.int32, (HGRN_SUB, HGRN_SUB), 1)
    same_step = jnp.right_shift(gi, 4) == jnp.right_shift(gj, 4)
    tri = jnp.where(same_step & (gj <= gi), 1.0, 0.0).astype(BF16)
    ones = jnp.ones((HGRN_HEAD, HGRN_HEAD), BF16)
    half = HGRN_STEP // 2
    trow = lax.broadcasted_iota(jnp.int32, (half, 1), 0)

    q = _silu(q_ref[rows, :].astype(F32))
    fg = lb + (1.0 - lb) * _sigmoid(f_ref[rows, :].astype(F32))
    kx = 1.0 - fg
    ic = i_ref[rows, :].astype(F32)
    kx_ref[...] = kx
    fg_ref[...] = fg
    ic_ref[...] = ic

    def row_bcast(ref, r):
        return jnp.broadcast_to(ref[r:r + 1, :], (half, HGRN_HEAD))

    bcs = _split_dot(tri, jnp.log(fg), 3)
    steps = range(HGRN_SUB // HGRN_STEP)
    rss = [slice(j * HGRN_STEP, (j + 1) * HGRN_STEP) for j in steps]
    bls = [bcs[rs][HGRN_STEP - 1:HGRN_STEP] for rs in rss]
    atts_lo, atts_hi = [], []
    for j, rs in enumerate(rss):
        qs, base = q[rs], j * HGRN_STEP
        q_lo, q_hi = qs[0:half], qs[half:HGRN_STEP]
        p_lo, p_hi = [None] * half, [None] * HGRN_STEP
        lo = hi = None
        for s in range(HGRN_STEP - 1, -1, -1):
            nxt = row_bcast(fg_ref, base + s + 1) if s + 1 < HGRN_STEP else None
            k_s = row_bcast(kx_ref, base + s)
            if s >= half:
                prev = 0.0 if hi is None else hi * nxt
                hi = jnp.where(trow == s - half, q_hi, prev)
            else:
                hi = hi * nxt
                prev = 0.0 if lo is None else lo * nxt
                lo = jnp.where(trow == s, q_lo, prev)
                p_lo[s] = lo * k_s
            p_hi[s] = hi * k_s
        stack = jnp.concatenate(p_lo + p_hi, axis=0).astype(BF16)
        att = _dot(stack, ones)
        atts_lo.append(att[0:half * half])
        atts_hi.append(att[half * half:])
    incs = [_dot_tn(ic[rs].astype(BF16), (kx[rs] * jnp.exp(bl - bcs[rs])).astype(BF16))
            for rs, bl in zip(rss, bls)]
    qb = (q * jnp.exp(bcs)).astype(BF16)
    st = state_ref[...]
    o_inters = []
    for j in steps:
        o_inters.append(_dot_nt(qb[rss[j]], st.astype(BF16)))
        st = st * jnp.exp(bls[j]) + incs[j]
    state_ref[...] = st
    outs = []
    for j, rs in enumerate(rss):
        a_lo, a_hi, base = atts_lo[j], atts_hi[j], j * HGRN_STEP
        o_lo = a_lo[0:half] * row_bcast(ic_ref, base)
        for s in range(1, half):
            o_lo = o_lo + a_lo[s * half:(s + 1) * half] * row_bcast(ic_ref, base + s)
        o_hi = a_hi[0:half] * row_bcast(ic_ref, base)
        for s in range(1, HGRN_STEP):
            o_hi = o_hi + a_hi[s * half:(s + 1) * half] * row_bcast(ic_ref, base + s)
        outs.append(o_inters[j] + jnp.concatenate([o_lo, o_hi], axis=0))
    o = jnp.concatenate(outs, axis=0)
    o = o * lax.rsqrt(jnp.mean(o * o, axis=-1, keepdims=True) + EPS) * g_ref[...]
    o_ref[rows, :] = (o * _silu(z_ref[rows, :].astype(F32))).astype(o_ref.dtype)


def _hgrn(hm, lb, g_norm, batch, seq):
    nt = seq // HGRN_TBLK
    col = lambda s: pl.BlockSpec(
        (HGRN_TBLK, HGRN_HEAD), lambda b, h, t: (b * nt + t, s * HGRN_HEADS + h))
    vec = pl.BlockSpec((1, HGRN_HEAD), lambda b, h, t: (0, h))
    return pl.pallas_call(
        _hgrn_kernel,
        grid=(batch, HGRN_HEADS, nt),
        in_specs=[col(SLAB_QC), col(SLAB_FC), col(SLAB_IC), col(SLAB_ZC), vec, vec],
        out_specs=pl.BlockSpec((HGRN_TBLK, HGRN_HEAD), lambda b, h, t: (b * nt + t, h)),
        out_shape=jax.ShapeDtypeStruct((batch * seq, MIX_W), BF16),
        scratch_shapes=[pltpu.VMEM((HGRN_HEAD, HGRN_HEAD), F32),
                        pltpu.VMEM((HGRN_SUB, HGRN_HEAD), F32),
                        pltpu.VMEM((HGRN_SUB, HGRN_HEAD), F32),
                        pltpu.VMEM((HGRN_SUB, HGRN_HEAD), F32)],
        compiler_params=pltpu.CompilerParams(
            dimension_semantics=("parallel", "parallel", "arbitrary"),
            vmem_limit_bytes=VMEM_LIMIT),
        name="hgrn",
    )(hm, hm, hm, hm, lb, g_norm)


def _merge_kernel(final, *refs):
    if final:
        oa, ob, oc, g0, g1, g2, x_ref, bm_ref, wb_ref, wo_ref, fg_ref, out_ref = refs
    else:
        oa, ob, oc, g0, g1, g2, x_ref, bm_ref, wb_ref, wo_ref, out_ref = refs
    y = None
    for n, (o_ref, gl_ref) in enumerate(((oa, g0), (ob, g1), (oc, g2))):
        proj = _dot(o_ref[...], wb_ref[n])
        term = _sigmoid(gl_ref[...].astype(F32) + bm_ref[n:n + 1, :]) * proj
        y = term if y is None else y + term
    xo = x_ref[...] + _dot(y.astype(BF16), wo_ref[...])
    if final:
        ms = jnp.mean(xo * xo, axis=-1, keepdims=True)
        xo = xo * lax.rsqrt(ms + EPS) * fg_ref[...]
    out_ref[...] = xo


def _merge(oa, ob, oc, hm, x2, bm, wb, wo, final_g, tm=512):
    m = x2.shape[0]
    tm = min(tm, m)
    final = final_g is not None
    act = pl.BlockSpec((tm, D_MODEL), lambda i: (i, 0))
    gl = lambda n: pl.BlockSpec((tm, D_MODEL), lambda i: (i, SLAB_GL + n))
    full = lambda a: pl.BlockSpec(a.shape, lambda i: (0,) * a.ndim)
    in_specs = [act, act, act, gl(0), gl(1), gl(2), act, full(bm), full(wb), full(wo)]
    args = [oa, ob, oc, hm, hm, hm, x2, bm, wb, wo]
    if final:
        in_specs.append(full(final_g))
        args.append(final_g)
    return pl.pallas_call(
        functools.partial(_merge_kernel, final),
        grid=(m // tm,),
        in_specs=in_specs,
        out_specs=act,
        out_shape=jax.ShapeDtypeStruct((m, D_MODEL), F32),
        compiler_params=pltpu.CompilerParams(
            dimension_semantics=("parallel",), vmem_limit_bytes=VMEM_LIMIT),
        name="merge",
    )(*args)


def kernel(x, norm_g, w_in, b_merge, mu_shift, w0, w_w2, a0, a_w2, v0, v_w1, v_w2, k_k, k_a, r_k,
           lnx_g, lnx_b, sgu_ln_g, sgu_ln_b, sgu_w, sgu_b, lb_logits, hgrn_g, w_branch, w_out,
           final_g):
    batch, seq, _ = x.shape
    depth = w_in.shape[0]
    assert seq % HGRN_TBLK == 0 and x.shape[2] == D_MODEL
    n_shift = 3 * MIX_W + 2 * R_LORA

    lb_all = jnp.cumsum(jax.nn.softmax(lb_logits.astype(F32), axis=0), axis=0)
    lb_all = lb_all - lb_all[0]

    x2 = x.reshape(batch * seq, D_MODEL)
    vfirst = None
    for l in range(depth):
        w = w_in[l]
        wm = jnp.concatenate([w[:, :3 * MIX_W], w[:, n_shift:]], axis=1).astype(BF16)
        ext = [w[:, 3 * MIX_W:n_shift]]
        if l > 0:
            ext.append(v_w1[l - 1])
        we = jnp.concatenate(ext, axis=1)
        we = jnp.pad(we, ((0, 0), (0, N_EXT - we.shape[1]))).astype(BF16)

        zrow = jnp.zeros((MIX_W,), F32)
        mu = mu_shift[l]
        rows = [mu[0:MIX_W], mu[MIX_W:2 * MIX_W], mu[2 * MIX_W:3 * MIX_W], w0[l], a0[l], k_k[l],
                k_a[l], r_k[l], lnx_g[l], lnx_b[l],
                v0[l - 1] if l > 0 else zrow,
                jnp.pad(mu[3 * MIX_W:n_shift], (0, MIX_W - 2 * R_LORA))]
        pvec = jnp.stack(rows + [zrow] * (16 - len(rows)), axis=0)
        zl = jnp.zeros((R_LORA, MIX_W), F32)
        w2 = jnp.concatenate(
            [jnp.concatenate([w_w2[l], zl], axis=0), jnp.concatenate([zl, a_w2[l]], axis=0)],
            axis=1).astype(BF16)

        hm, he = _proj_in(x2, norm_g[l][None, :], wm, we)

        if l == 0:
            oa, vfirst = _rwkv(hm, he, None, pvec, w2, None, batch, seq)
        else:
            vw2 = jnp.pad(v_w2[l - 1], ((0, LANES - R_V), (0, 0))).astype(BF16)
            oa = _rwkv(hm, he, vfirst, pvec, w2, vw2, batch, seq)

        bias = jnp.repeat(sgu_b[l].T, LANES, axis=1)
        ob = _sgu(hm, sgu_ln_g[l][None, :], sgu_ln_b[l][None, :], sgu_w[l], bias, batch, seq)
        oc = _hgrn(hm, lb_all[l][None, :], hgrn_g[l][None, :], batch, seq)

        bm = jnp.pad(b_merge[l], ((0, 8 - N_BRANCH), (0, 0)))
        x2 = _merge(oa, ob, oc, hm, x2, bm, w_branch[l].astype(BF16), w_out[l].astype(BF16),
                    final_g[None, :] if l == depth - 1 else None)
    return x2.reshape(batch, seq, D_MODEL)
```

```python
import functools
import math

import jax
import jax.numpy as jnp
from jax import lax
from jax.experimental import pallas as pl
from jax.experimental.pallas import tpu as pltpu

F32 = jnp.float32
BF16 = jnp.bfloat16

D_MODEL = 1024
MIX_W = 1024
N_BRANCH = 3
CHUNK = 64
RWKV_TBLK = 256
RWKV_GRID_TBLK = 1024
RWKV_HEAD_W = 64
R_LORA = 64
R_V = 32
SGU_BLOCK = 128
SGU_TBLK = 512
SGU_GROUPS = 8
HGRN_HEAD = 128
HGRN_HEADS = MIX_W // HGRN_HEAD
HGRN_STEP = 16
HGRN_SUB = 512
HGRN_TBLK = 4096
EPS = 1e-6
GN_EPS = 64e-5
LANES = 128
GROUP_W = 256
N_GROUPS = MIX_W // GROUP_W
N_MAIN = 14 * 1024
N_EXT = 256
VMEM_LIMIT = 48 * 1024 * 1024

SLAB_R, SLAB_K, SLAB_V, SLAB_ZA, SLAB_UB, SLAB_VB, SLAB_ZB = 0, 1, 2, 3, 4, 5, 6
SLAB_QC, SLAB_FC, SLAB_IC, SLAB_ZC, SLAB_GL = 7, 8, 9, 10, 11


def _dot(a, b):
    return jnp.dot(a, b, preferred_element_type=F32)


def _dot_nt(a, b):
    return lax.dot_general(a, b, (((1,), (1,)), ((), ())), preferred_element_type=F32)


def _dot_tn(a, b):
    return lax.dot_general(a, b, (((0,), (0,)), ((), ())), preferred_element_type=F32)


def _split_dot(sel, x, terms):
    acc = None
    rem = x
    for _ in range(terms):
        part = rem.astype(BF16)
        d = _dot(sel, part)
        acc = d if acc is None else acc + d
        rem = rem - part.astype(F32)
    return acc


def _sigmoid(x):
    return jax.nn.sigmoid(x)


def _silu(x):
    return x * jax.nn.sigmoid(x)


def _proj_kernel(x_ref, g_ref, wm_ref, we_ref, hm_ref, he_ref, xn_ref):
    @pl.when(pl.program_id(1) == 0)
    def _():
        x = x_ref[...]
        ms = jnp.mean(x * x, axis=-1, keepdims=True)
        xn = (x * lax.rsqrt(ms + EPS) * g_ref[...]).astype(BF16)
        xn_ref[...] = xn
        he_ref[...] = _dot(xn, we_ref[...])

    hm_ref[...] = _dot(xn_ref[...], wm_ref[...]).astype(hm_ref.dtype)


def _proj_in(x2, g, wm, we, tm=2048, tn=1024):
    m = x2.shape[0]
    tm = min(tm, m)
    return pl.pallas_call(
        _proj_kernel,
        grid=(m // tm, N_MAIN // tn),
        in_specs=[
            pl.BlockSpec((tm, D_MODEL), lambda i, j: (i, 0)),
            pl.BlockSpec((1, D_MODEL), lambda i, j: (0, 0)),
            pl.BlockSpec((D_MODEL, tn), lambda i, j: (0, j)),
            pl.BlockSpec((D_MODEL, N_EXT), lambda i, j: (0, 0)),
        ],
        out_specs=[
            pl.BlockSpec((tm, tn), lambda i, j: (i, j)),
            pl.BlockSpec((tm, N_EXT), lambda i, j: (i, 0)),
        ],
        out_shape=[
            jax.ShapeDtypeStruct((m, N_MAIN), BF16),
            jax.ShapeDtypeStruct((m, N_EXT), F32),
        ],
        scratch_shapes=[pltpu.VMEM((tm, D_MODEL), BF16)],
        compiler_params=pltpu.CompilerParams(
            dimension_semantics=("parallel", "arbitrary"), vmem_limit_bytes=VMEM_LIMIT),
        name="proj_in",
    )(x2, g, wm, we)


def _rwkv_kernel(has_vres, n_blk, *refs):
    scratch = refs[-5:]

    @pl.when(pl.program_id(1) == 0)
    def _():
        for ref in scratch:
            ref[...] = jnp.zeros_like(ref)

    def block(blk, carry):
        _rwkv_block(has_vres, pl.ds(pl.multiple_of(blk * RWKV_TBLK, RWKV_TBLK), RWKV_TBLK), refs)
        return carry

    lax.fori_loop(0, n_blk, block, 0)


def _rwkv_block(has_vres, rows, refs):
    if has_vres:
        (r_ref, k_ref, v_ref, z_ref, wa_ref, vl_ref, vf_ref, pv_ref, w2_ref, vw2_ref,
         o_ref, state_ref, cr_ref, ck_ref, cv_ref, cwa_ref) = refs
    else:
        (r_ref, k_ref, v_ref, z_ref, wa_ref, pv_ref, w2_ref,
         o_ref, vfo_ref, state_ref, cr_ref, ck_ref, cv_ref, cwa_ref) = refs

    row = lax.broadcasted_iota(jnp.int32, (RWKV_TBLK, 1), 0)

    def shift_mix(x_ref, c_ref, mu):
        x = x_ref[rows, :].astype(F32)
        prev = c_ref[7:8, :]
        xs = jnp.where(row == 0, prev, pltpu.roll(x, 1, 0))
        c_ref[...] = x[RWKV_TBLK - 8:RWKV_TBLK, :]
        return x + mu * (xs - x)

    pv = pv_ref[...]
    mu_r, mu_k, mu_v = pv[0:1], pv[1:2], pv[2:3]
    w0, a0, k_k, k_a, r_k = pv[3:4], pv[4:5], pv[5:6], pv[6:7], pv[7:8]
    lnx_g, lnx_b, v0 = pv[8:9], pv[9:10], pv[10:11]
    mu_wa = pv[11:12, 0:LANES]

    r = shift_mix(r_ref, cr_ref, mu_r)
    k = shift_mix(k_ref, ck_ref, mu_k)
    v = shift_mix(v_ref, cv_ref, mu_v)
    wa = shift_mix(wa_ref, cwa_ref, mu_wa)

    lane = lax.broadcasted_iota(jnp.int32, (1, LANES), 1)
    tl = jnp.where(lane < R_LORA, jnp.tanh(wa), wa).astype(BF16)
    pre = _dot(tl, w2_ref[...])
    lw = -math.exp(-0.5) * _sigmoid(w0 + pre[:, :MIX_W])
    a = _sigmoid(a0 + pre[:, MIX_W:])

    if has_vres:
        gate = _sigmoid(v0 + _dot(vl_ref[rows, :].astype(BF16), vw2_ref[...]))
        v = v + (vf_ref[rows, :] - v) * gate
    else:
        vfo_ref[rows, :] = v

    ti = lax.broadcasted_iota(jnp.int32, (RWKV_TBLK, RWKV_TBLK), 0)
    si = lax.broadcasted_iota(jnp.int32, (RWKV_TBLK, RWKV_TBLK), 1)
    same_chunk = jnp.right_shift(ti, 6) == jnp.right_shift(si, 6)
    tri = jnp.where(same_chunk & (si <= ti), 1.0, 0.0).astype(BF16)
    cum = _split_dot(tri, lw, 3)
    ecum = jnp.exp(cum)
    ecum_prev = jnp.exp(cum - lw)
    einv = jnp.exp(-cum)
    chunks = range(RWKV_TBLK // CHUNK)
    rcs = [slice(c * CHUNK, (c + 1) * CHUNK) for c in chunks]
    cls = [cum[(c + 1) * CHUNK - 1:(c + 1) * CHUNK] for c in chunks]
    g_cs = [jnp.exp(cl) for cl in cls]

    groups = range(N_GROUPS)
    gls = [slice(g * GROUP_W, (g + 1) * GROUP_W) for g in groups]
    units = [(c, g) for c in chunks for g in groups]
    t_i = lax.broadcasted_iota(jnp.int32, (CHUNK, GROUP_W), 0)
    l_i = lax.broadcasted_iota(jnp.int32, (CHUNK, GROUP_W), 1)
    s_i = jnp.bitwise_and(l_i, RWKV_HEAD_W - 1)
    strict = t_i > s_i
    incl = t_i >= s_i
    eye = jnp.where(t_i == s_i, 1.0, 0.0).astype(F32)
    head_of_lane = jnp.right_shift(l_i, 6)
    head_masks = [jnp.where(head_of_lane == h, 1.0, 0.0).astype(BF16)
                  for h in range(GROUP_W // RWKV_HEAD_W)]
    bi = lax.broadcasted_iota(jnp.int32, (GROUP_W, GROUP_W), 0)
    bj = lax.broadcasted_iota(jnp.int32, (GROUP_W, GROUP_W), 1)
    same_head = jnp.right_shift(bi, 6) == jnp.right_shift(bj, 6)
    seg = jnp.where(same_head, 1.0, 0.0).astype(BF16)

    def stack4(x):
        xb = x.astype(BF16)
        return jnp.concatenate([xb * m for m in head_masks], axis=0)

    def segsum(x):
        return _dot(x.astype(BF16), seg)

    kk_raw = k * k_k
    k2 = k * (1.0 + (a - 1.0) * k_a)
    sums2 = [segsum(jnp.concatenate([kk_raw[:, gl] * kk_raw[:, gl],
                                     r[:, gl] * k2[:, gl] * r_k[:, gl]], axis=0)) for gl in gls]
    lhss, rhss, bks, v_sts = {}, {}, {}, {}
    for c, g in units:
        rc, gl = rcs[c], gls[g]
        kk = kk_raw[rc, gl] * lax.rsqrt(jnp.maximum(sums2[g][rc], 1e-24))
        kka = kk * a[rc, gl]
        at = -kk * ecum_prev[rc, gl]
        bt = kka * einv[rc, gl]
        kt = k2[rc, gl] * einv[rc, gl]
        rt = r[rc, gl] * ecum[rc, gl]
        lhss[c, g] = jnp.concatenate([at, rt], axis=0).astype(BF16)
        rhss[c, g] = jnp.concatenate([stack4(bt), stack4(kt)], axis=0)
        bks[c, g] = jnp.concatenate([bt, kt], axis=0).astype(BF16)
        v_sts[c, g] = stack4(v[rc, gl])

    gs = {u: _dot_nt(lhss[u], rhss[u]) for u in units}
    n_abs = {u: jnp.where(strict, gs[u][0:CHUNK, 0:GROUP_W], 0.0) for u in units}
    g_ks = {u: jnp.concatenate([jnp.where(strict, gs[u][0:CHUNK, GROUP_W:], 0.0),
                                jnp.where(incl, gs[u][CHUNK:, GROUP_W:], 0.0)],
                               axis=0).astype(BF16) for u in units}
    g_rbs = {u: jnp.where(incl, gs[u][CHUNK:, 0:GROUP_W], 0.0).astype(BF16) for u in units}
    gkvs = {u: _dot(g_ks[u], v_sts[u]) for u in units}

    psts = {u: stack4(n_abs[u]) for u in units}
    pws = {u: _dot(n_abs[u].astype(BF16), psts[u]) for u in units}
    sks = {u: eye + n_abs[u] for u in units}
    for _ in range(4):
        psts = {u: stack4(pws[u]) for u in units}
        prods = {u: _dot(jnp.concatenate([sks[u], pws[u]], axis=0).astype(BF16), psts[u])
                 for u in units}
        sks = {u: sks[u] + prods[u][0:CHUNK] for u in units}
        pws = {u: prods[u][CHUNK:] for u in units}
    psts = {u: stack4(pws[u]) for u in units}
    tinvs = {u: (sks[u] + _dot(sks[u].astype(BF16), psts[u])).astype(BF16) for u in units}

    states = [state_ref[g] for g in groups]
    ys = {}
    for c in chunks:
        ahs = [_dot_nt(lhss[c, g], states[g].astype(BF16)) for g in groups]
        us = [_dot(tinvs[c, g], stack4(ahs[g][0:CHUNK] + gkvs[c, g][0:CHUNK])) for g in groups]
        for g in groups:
            ys[c, g] = (ahs[g][CHUNK:] + gkvs[c, g][CHUNK:]
                        + _dot(g_rbs[c, g], stack4(us[g])))
        uvs = [jnp.concatenate([us[g], v[rcs[c], gls[g]]], axis=0).astype(BF16) for g in groups]
        states = [(states[g] + jnp.where(same_head, _dot_tn(uvs[g], bks[c, g]), 0.0))
                  * g_cs[c][:, gls[g]] for g in groups]
    state_ref[...] = jnp.stack(states, axis=0)

    y_rows = [jnp.concatenate([ys[c, g] for c in chunks], axis=0) for g in groups]
    means = [segsum(y) * (1.0 / RWKV_HEAD_W) for y in y_rows]
    ds = [y_rows[g] - means[g] for g in groups]
    variances = [segsum(d * d) * (1.0 / RWKV_HEAD_W) for d in ds]
    outs = []
    for g, gl in enumerate(gls):
        yn = ds[g] * lax.rsqrt(variances[g] + GN_EPS) * lnx_g[:, gl] + lnx_b[:, gl]
        bonus = sums2[g][RWKV_TBLK:] * v[:, gl]
        outs.append((yn + bonus) * _silu(z_ref[rows, gl].astype(F32)))
    o_ref[rows, :] = jnp.concatenate(outs, axis=1).astype(o_ref.dtype)


def _rwkv(hm, he, vfirst, pvec, w2, vw2, batch, seq):
    tblk = min(RWKV_GRID_TBLK, seq)
    nt = seq // tblk
    has_vres = vfirst is not None
    tok = lambda b, t: b * nt + t
    slab = lambda s: pl.BlockSpec((tblk, MIX_W), lambda b, t: (tok(b, t), s))
    ext = lambda s: pl.BlockSpec((tblk, LANES), lambda b, t: (tok(b, t), s))
    full = lambda a: pl.BlockSpec(a.shape, lambda b, t: (0,) * a.ndim)
    act = pl.BlockSpec((tblk, MIX_W), lambda b, t: (tok(b, t), 0))
    in_specs = [slab(SLAB_R), slab(SLAB_K), slab(SLAB_V), slab(SLAB_ZA), ext(0)]
    args = [hm, hm, hm, hm, he]
    if has_vres:
        in_specs += [ext(1), act, full(pvec), full(w2), full(vw2)]
        args += [he, vfirst, pvec, w2, vw2]
        out_specs = act
        out_shape = jax.ShapeDtypeStruct((batch * seq, MIX_W), BF16)
    else:
        in_specs += [full(pvec), full(w2)]
        args += [pvec, w2]
        out_specs = [act, act]
        out_shape = [jax.ShapeDtypeStruct((batch * seq, MIX_W), BF16),
                     jax.ShapeDtypeStruct((batch * seq, MIX_W), F32)]
    return pl.pallas_call(
        functools.partial(_rwkv_kernel, has_vres, tblk // RWKV_TBLK),
        grid=(batch, nt),
        in_specs=in_specs,
        out_specs=out_specs,
        out_shape=out_shape,
        scratch_shapes=[
            pltpu.VMEM((N_GROUPS, GROUP_W, GROUP_W), F32),
            pltpu.VMEM((8, MIX_W), F32),
            pltpu.VMEM((8, MIX_W), F32),
            pltpu.VMEM((8, MIX_W), F32),
            pltpu.VMEM((8, LANES), F32),
        ],
        compiler_params=pltpu.CompilerParams(
            dimension_semantics=("parallel", "arbitrary"), vmem_limit_bytes=VMEM_LIMIT),
        name="rwkv",
    )(*args)


def _sgu_kernel(u_ref, v_ref, z_ref, lng_ref, lnb_ref, ws_ref, bias_ref, o_ref):
    v = v_ref[...].astype(F32)
    mean = jnp.mean(v, axis=-1, keepdims=True)
    d = v - mean
    var = jnp.mean(d * d, axis=-1, keepdims=True)
    vn = (d * lax.rsqrt(var + EPS) * lng_ref[...] + lnb_ref[...]).astype(BF16)
    ti = lax.broadcasted_iota(jnp.int32, (SGU_BLOCK, SGU_BLOCK), 0)
    si = lax.broadcasted_iota(jnp.int32, (SGU_BLOCK, SGU_BLOCK), 1)
    causal = (ti >= CHUNK) | (si < CHUNK)
    for g in range(SGU_GROUPS):
        sl = slice(g * LANES, (g + 1) * LANES)
        ws = jnp.where(causal, ws_ref[g], 0.0).astype(BF16)
        for b in range(SGU_TBLK // SGU_BLOCK):
            rows = slice(b * SGU_BLOCK, (b + 1) * SGU_BLOCK)
            mixed = _dot(ws, vn[rows, sl]) + bias_ref[:, sl]
            o_ref[rows, sl] = (u_ref[rows, sl].astype(F32) * mixed
                               * _silu(z_ref[rows, sl].astype(F32))).astype(o_ref.dtype)


def _sgu(hm, ln_g, ln_b, ws, bias, batch, seq):
    nb = seq // SGU_TBLK
    slab = lambda s: pl.BlockSpec((SGU_TBLK, MIX_W), lambda b, t: (b * nb + t, s))
    full = lambda a: pl.BlockSpec(a.shape, lambda b, t: (0,) * a.ndim)
    return pl.pallas_call(
        _sgu_kernel,
        grid=(batch, nb),
        in_specs=[slab(SLAB_UB), slab(SLAB_VB), slab(SLAB_ZB),
                  full(ln_g), full(ln_b), full(ws), full(bias)],
        out_specs=pl.BlockSpec((SGU_TBLK, MIX_W), lambda b, t: (b * nb + t, 0)),
        out_shape=jax.ShapeDtypeStruct((batch * seq, MIX_W), BF16),
        compiler_params=pltpu.CompilerParams(
            dimension_semantics=("parallel", "parallel"), vmem_limit_bytes=VMEM_LIMIT),
        name="sgu",
    )(hm, hm, hm, ln_g, ln_b, ws, bias)


def _hgrn_kernel(n_sub, q_ref, f_ref, i_ref, z_ref, lb_ref, g_ref, o_ref, state_ref, kx_ref, fg_ref,
                 ic_ref):
    @pl.when(pl.program_id(2) == 0)
    def _():
        state_ref[...] = jnp.zeros_like(state_ref)

    def sub_block(sub, carry):
        rows = pl.ds(pl.multiple_of(sub * HGRN_SUB, HGRN_SUB), HGRN_SUB)
        _hgrn_block(rows, q_ref, f_ref, i_ref, z_ref, lb_ref, g_ref, o_ref, state_ref, kx_ref,
                    fg_ref, ic_ref)
        return carry

    lax.fori_loop(0, n_sub, sub_block, 0)


def _hgrn_block(rows, q_ref, f_ref, i_ref, z_ref, lb_ref, g_ref, o_ref, state_ref, kx_ref, fg_ref,
                ic_ref):
    lb = lb_ref[...]
    gi = lax.broadcasted_iota(jnp.int32, (HGRN_SUB, HGRN_SUB), 0)
    gj = lax.broadcasted_iota(jnp.int32, (HGRN_SUB, HGRN_SUB), 1)
    same_step = jnp.right_shift(gi, 4) == jnp.right_shift(gj, 4)
    tri = jnp.where(same_step & (gj <= gi), 1.0, 0.0).astype(BF16)
    ones = jnp.ones((HGRN_HEAD, HGRN_HEAD), BF16)
    half = HGRN_STEP // 2
    trow = lax.broadcasted_iota(jnp.int32, (half, 1), 0)

    q = _silu(q_ref[rows, :].astype(F32))
    fg = lb + (1.0 - lb) * _sigmoid(f_ref[rows, :].astype(F32))
    kx = 1.0 - fg
    ic = i_ref[rows, :].astype(F32)
    kx_ref[...] = kx
    fg_ref[...] = fg
    ic_ref[...] = ic

    def row_bcast(ref, r):
        return jnp.broadcast_to(ref[r:r + 1, :], (half, HGRN_HEAD))

    bcs = _split_dot(tri, jnp.log(fg), 3)
    steps = range(HGRN_SUB // HGRN_STEP)
    rss = [slice(j * HGRN_STEP, (j + 1) * HGRN_STEP) for j in steps]
    bls = [bcs[rs][HGRN_STEP - 1:HGRN_STEP] for rs in rss]
    atts_lo, atts_hi = [], []
    for j, rs in enumerate(rss):
        qs, base = q[rs], j * HGRN_STEP
        q_lo, q_hi = qs[0:half], qs[half:HGRN_STEP]
        p_lo, p_hi = [None] * half, [None] * HGRN_STEP
        lo = hi = None
        for s in range(HGRN_STEP - 1, -1, -1):
            nxt = row_bcast(fg_ref, base + s + 1) if s + 1 < HGRN_STEP else None
            k_s = row_bcast(kx_ref, base + s)
            if s >= half:
                prev = 0.0 if hi is None else hi * nxt
                hi = jnp.where(trow == s - half, q_hi, prev)
            else:
                hi = hi * nxt
                prev = 0.0 if lo is None else lo * nxt
                lo = jnp.where(trow == s, q_lo, prev)
                p_lo[s] = lo * k_s
            p_hi[s] = hi * k_s
        stack = jnp.concatenate(p_lo + p_hi, axis=0).astype(BF16)
        att = _dot(stack, ones)
        atts_lo.append(att[0:half * half])
        atts_hi.append(att[half * half:])
    incs = [_dot_tn(ic[rs].astype(BF16), (kx[rs] * jnp.exp(bl - bcs[rs])).astype(BF16))
            for rs, bl in zip(rss, bls)]
    qb = (q * jnp.exp(bcs)).astype(BF16)
    st = state_ref[...]
    o_inters = []
    for j in steps:
        o_inters.append(_dot_nt(qb[rss[j]], st.astype(BF16)))
        st = st * jnp.exp(bls[j]) + incs[j]
    state_ref[...] = st
    outs = []
    for j, rs in enumerate(rss):
        a_lo, a_hi, base = atts_lo[j], atts_hi[j], j * HGRN_STEP
        o_lo = a_lo[0:half] * row_bcast(ic_ref, base)
        for s in range(1, half):
            o_lo = o_lo + a_lo[s * half:(s + 1) * half] * row_bcast(ic_ref, base + s)
        o_hi = a_hi[0:half] * row_bcast(ic_ref, base)
        for s in range(1, HGRN_STEP):
            o_hi = o_hi + a_hi[s * half:(s + 1) * half] * row_bcast(ic_ref, base + s)
        outs.append(o_inters[j] + jnp.concatenate([o_lo, o_hi], axis=0))
    o = jnp.concatenate(outs, axis=0)
    o = o * lax.rsqrt(jnp.mean(o * o, axis=-1, keepdims=True) + EPS) * g_ref[...]
    o_ref[rows, :] = (o * _silu(z_ref[rows, :].astype(F32))).astype(o_ref.dtype)


def _hgrn(hm, lb, g_norm, batch, seq):
    tblk = min(HGRN_TBLK, seq)
    nt = seq // tblk
    col = lambda s: pl.BlockSpec(
        (tblk, HGRN_HEAD), lambda b, h, t: (b * nt + t, s * HGRN_HEADS + h))
    vec = pl.BlockSpec((1, HGRN_HEAD), lambda b, h, t: (0, h))
    return pl.pallas_call(
        functools.partial(_hgrn_kernel, tblk // HGRN_SUB),
        grid=(batch, HGRN_HEADS, nt),
        in_specs=[col(SLAB_QC), col(SLAB_FC), col(SLAB_IC), col(SLAB_ZC), vec, vec],
        out_specs=pl.BlockSpec((tblk, HGRN_HEAD), lambda b, h, t: (b * nt + t, h)),
        out_shape=jax.ShapeDtypeStruct((batch * seq, MIX_W), BF16),
        scratch_shapes=[pltpu.VMEM((HGRN_HEAD, HGRN_HEAD), F32),
                        pltpu.VMEM((HGRN_SUB, HGRN_HEAD), F32),
                        pltpu.VMEM((HGRN_SUB, HGRN_HEAD), F32),
                        pltpu.VMEM((HGRN_SUB, HGRN_HEAD), F32)],
        compiler_params=pltpu.CompilerParams(
            dimension_semantics=("parallel", "parallel", "arbitrary"),
            vmem_limit_bytes=VMEM_LIMIT),
        name="hgrn",
    )(hm, hm, hm, hm, lb, g_norm)


def _merge_kernel(final, *refs):
    if final:
        oa, ob, oc, g0, g1, g2, x_ref, bm_ref, wb_ref, wo_ref, fg_ref, out_ref = refs
    else:
        oa, ob, oc, g0, g1, g2, x_ref, bm_ref, wb_ref, wo_ref, out_ref = refs
    y = None
    for n, (o_ref, gl_ref) in enumerate(((oa, g0), (ob, g1), (oc, g2))):
        proj = _dot(o_ref[...], wb_ref[n])
        term = _sigmoid(gl_ref[...].astype(F32) + bm_ref[n:n + 1, :]) * proj
        y = term if y is None else y + term
    xo = x_ref[...] + _dot(y.astype(BF16), wo_ref[...])
    if final:
        ms = jnp.mean(xo * xo, axis=-1, keepdims=True)
        xo = xo * lax.rsqrt(ms + EPS) * fg_ref[...]
    out_ref[...] = xo


def _merge(oa, ob, oc, hm, x2, bm, wb, wo, final_g, tm=512):
    m = x2.shape[0]
    tm = min(tm, m)
    final = final_g is not None
    act = pl.BlockSpec((tm, D_MODEL), lambda i: (i, 0))
    gl = lambda n: pl.BlockSpec((tm, D_MODEL), lambda i: (i, SLAB_GL + n))
    full = lambda a: pl.BlockSpec(a.shape, lambda i: (0,) * a.ndim)
    in_specs = [act, act, act, gl(0), gl(1), gl(2), act, full(bm), full(wb), full(wo)]
    args = [oa, ob, oc, hm, hm, hm, x2, bm, wb, wo]
    if final:
        in_specs.append(full(final_g))
        args.append(final_g)
    return pl.pallas_call(
        functools.partial(_merge_kernel, final),
        grid=(m // tm,),
        in_specs=in_specs,
        out_specs=act,
        out_shape=jax.ShapeDtypeStruct((m, D_MODEL), F32),
        compiler_params=pltpu.CompilerParams(
            dimension_semantics=("parallel",), vmem_limit_bytes=VMEM_LIMIT),
        name="merge",
    )(*args)


def kernel(x, norm_g, w_in, b_merge, mu_shift, w0, w_w2, a0, a_w2, v0, v_w1, v_w2, k_k, k_a, r_k,
           lnx_g, lnx_b, sgu_ln_g, sgu_ln_b, sgu_w, sgu_b, lb_logits, hgrn_g, w_branch, w_out,
           final_g):
    batch, seq, _ = x.shape
    depth = w_in.shape[0]
    assert seq % min(HGRN_TBLK, seq) == 0 and seq % HGRN_SUB == 0 and x.shape[2] == D_MODEL
    assert seq % min(RWKV_GRID_TBLK, seq) == 0 and seq % RWKV_TBLK == 0
    n_shift = 3 * MIX_W + 2 * R_LORA

    lb_all = jnp.cumsum(jax.nn.softmax(lb_logits.astype(F32), axis=0), axis=0)
    lb_all = lb_all - lb_all[0]

    x2 = x.reshape(batch * seq, D_MODEL)
    vfirst = None
    for l in range(depth):
        w = w_in[l]
        wm = jnp.concatenate([w[:, :3 * MIX_W], w[:, n_shift:]], axis=1).astype(BF16)
        ext = [w[:, 3 * MIX_W:n_shift]]
        if l > 0:
            ext.append(v_w1[l - 1])
        we = jnp.concatenate(ext, axis=1)
        we = jnp.pad(we, ((0, 0), (0, N_EXT - we.shape[1]))).astype(BF16)

        zrow = jnp.zeros((MIX_W,), F32)
        mu = mu_shift[l]
        rows = [mu[0:MIX_W], mu[MIX_W:2 * MIX_W], mu[2 * MIX_W:3 * MIX_W], w0[l], a0[l], k_k[l],
                k_a[l], r_k[l], lnx_g[l], lnx_b[l],
                v0[l - 1] if l > 0 else zrow,
                jnp.pad(mu[3 * MIX_W:n_shift], (0, MIX_W - 2 * R_LORA))]
        pvec = jnp.stack(rows + [zrow] * (16 - len(rows)), axis=0)
        zl = jnp.zeros((R_LORA, MIX_W), F32)
        w2 = jnp.concatenate(
            [jnp.concatenate([w_w2[l], zl], axis=0), jnp.concatenate([zl, a_w2[l]], axis=0)],
            axis=1).astype(BF16)

        hm, he = _proj_in(x2, norm_g[l][None, :], wm, we)

        if l == 0:
            oa, vfirst = _rwkv(hm, he, None, pvec, w2, None, batch, seq)
        else:
            vw2 = jnp.pad(v_w2[l - 1], ((0, LANES - R_V), (0, 0))).astype(BF16)
            oa = _rwkv(hm, he, vfirst, pvec, w2, vw2, batch, seq)

        bias = jnp.repeat(sgu_b[l].T, LANES, axis=1)
        ob = _sgu(hm, sgu_ln_g[l][None, :], sgu_ln_b[l][None, :], sgu_w[l], bias, batch, seq)
        oc = _hgrn(hm, lb_all[l][None, :], hgrn_g[l][None, :], batch, seq)

        bm = jnp.pad(b_merge[l], ((0, 8 - N_BRANCH), (0, 0)))
        x2 = _merge(oa, ob, oc, hm, x2, bm, w_branch[l].astype(BF16), w_out[l].astype(BF16),
                    final_g[None, :] if l == depth - 1 else None)
    return x2.reshape(batch, seq, D_MODEL)
```

```python
import functools
import math

import jax
import jax.numpy as jnp
from jax import lax
from jax.experimental import pallas as pl
from jax.experimental.pallas import tpu as pltpu

F32 = jnp.float32
BF16 = jnp.bfloat16

D_MODEL = 1024
MIX_W = 1024
N_BRANCH = 3
CHUNK = 64
CHUNK_LOG2 = 6
RWKV_TBLK = 256
RWKV_HEAD_W = 64
R_LORA = 64
R_V = 32
SGU_BLOCK = 128
SGU_TBLK = 512
SGU_GROUPS = 8
HGRN_HEAD = 128
HGRN_HEADS = MIX_W // HGRN_HEAD
HGRN_STEP = 16
HGRN_STEP_LOG2 = 4
HGRN_SUB = 512
HGRN_TBLK = 4096
EPS = 1e-6
GN_EPS = 64e-5
LANES = 128
SUBLANES = 8
GROUP_W = 256
N_GROUPS = MIX_W // GROUP_W
N_MAIN = 14 * 1024
N_EXT = 256
VMEM_LIMIT = 48 * 1024 * 1024
PROJ_TM, PROJ_TN = 2048, 1024
MERGE_TM = 512
KK_NORM_FLOOR_SQ = 1e-24
(PV_MU_R, PV_MU_K, PV_MU_V, PV_W0, PV_A0, PV_KK, PV_KA, PV_RK, PV_LNG, PV_LNB, PV_V0,
 PV_MU_WA, PV_ROWS) = range(12 + 1)

SLAB_R, SLAB_K, SLAB_V, SLAB_ZA, SLAB_UB, SLAB_VB, SLAB_ZB = 0, 1, 2, 3, 4, 5, 6
SLAB_QC, SLAB_FC, SLAB_IC, SLAB_ZC, SLAB_GL = 7, 8, 9, 10, 11


def _dot(a, b):
    return jnp.dot(a, b, preferred_element_type=F32)


def _dot_nt(a, b):
    return lax.dot_general(a, b, (((1,), (1,)), ((), ())), preferred_element_type=F32)


def _dot_tn(a, b):
    return lax.dot_general(a, b, (((0,), (0,)), ((), ())), preferred_element_type=F32)


def _split_dot(sel, x, terms):
    acc = None
    rem = x
    for _ in range(terms):
        part = rem.astype(BF16)
        d = _dot(sel, part)
        acc = d if acc is None else acc + d
        rem = rem - part.astype(F32)
    return acc


def _sigmoid(x):
    return jax.nn.sigmoid(x)


def _silu(x):
    return x * jax.nn.sigmoid(x)


def _proj_kernel(x_ref, g_ref, wm_ref, we_ref, hm_ref, he_ref, xn_ref):
    @pl.when(pl.program_id(1) == 0)
    def _():
        x = x_ref[...]
        ms = jnp.mean(x * x, axis=-1, keepdims=True)
        xn = (x * lax.rsqrt(ms + EPS) * g_ref[...]).astype(BF16)
        xn_ref[...] = xn
        he_ref[...] = _dot(xn, we_ref[...])

    hm_ref[...] = _dot(xn_ref[...], wm_ref[...]).astype(hm_ref.dtype)


def _proj_in(x2, g, wm, we):
    m = x2.shape[0]
    tm, tn = min(PROJ_TM, m), PROJ_TN
    return pl.pallas_call(
        _proj_kernel,
        grid=(m // tm, N_MAIN // tn),
        in_specs=[
            pl.BlockSpec((tm, D_MODEL), lambda i, j: (i, 0)),
            pl.BlockSpec((1, D_MODEL), lambda i, j: (0, 0)),
            pl.BlockSpec((D_MODEL, tn), lambda i, j: (0, j)),
            pl.BlockSpec((D_MODEL, N_EXT), lambda i, j: (0, 0)),
        ],
        out_specs=[
            pl.BlockSpec((tm, tn), lambda i, j: (i, j)),
            pl.BlockSpec((tm, N_EXT), lambda i, j: (i, 0)),
        ],
        out_shape=[
            jax.ShapeDtypeStruct((m, N_MAIN), BF16),
            jax.ShapeDtypeStruct((m, N_EXT), F32),
        ],
        scratch_shapes=[pltpu.VMEM((tm, D_MODEL), BF16)],
        compiler_params=pltpu.CompilerParams(
            dimension_semantics=("parallel", "arbitrary"), vmem_limit_bytes=VMEM_LIMIT),
        name="proj_in",
    )(x2, g, wm, we)


def _rwkv_kernel(has_vres, *refs):
    if has_vres:
        (r_ref, k_ref, v_ref, z_ref, wa_ref, vl_ref, vf_ref, pv_ref, w2_ref, vw2_ref,
         o_ref, state_ref, cr_ref, ck_ref, cv_ref, cwa_ref) = refs
    else:
        (r_ref, k_ref, v_ref, z_ref, wa_ref, pv_ref, w2_ref,
         o_ref, vfo_ref, state_ref, cr_ref, ck_ref, cv_ref, cwa_ref) = refs

    @pl.when(pl.program_id(1) == 0)
    def _():
        for ref in (state_ref, cr_ref, ck_ref, cv_ref, cwa_ref):
            ref[...] = jnp.zeros_like(ref)

    row = lax.broadcasted_iota(jnp.int32, (RWKV_TBLK, 1), 0)

    def shift_mix(x_ref, c_ref, mu):
        x = x_ref[...].astype(F32)
        prev = c_ref[SUBLANES - 1:SUBLANES, :]
        xs = jnp.where(row == 0, prev, pltpu.roll(x, 1, 0))
        c_ref[...] = x[RWKV_TBLK - SUBLANES:RWKV_TBLK, :]
        return x + mu * (xs - x)

    pv = pv_ref[...]
    prow = lambda i: pv[i:i + 1]
    mu_r, mu_k, mu_v = prow(PV_MU_R), prow(PV_MU_K), prow(PV_MU_V)
    w0, a0, k_k, k_a, r_k = prow(PV_W0), prow(PV_A0), prow(PV_KK), prow(PV_KA), prow(PV_RK)
    lnx_g, lnx_b, v0 = prow(PV_LNG), prow(PV_LNB), prow(PV_V0)
    mu_wa = prow(PV_MU_WA)[:, 0:LANES]

    r = shift_mix(r_ref, cr_ref, mu_r)
    k = shift_mix(k_ref, ck_ref, mu_k)
    v = shift_mix(v_ref, cv_ref, mu_v)
    wa = shift_mix(wa_ref, cwa_ref, mu_wa)

    lane = lax.broadcasted_iota(jnp.int32, (1, LANES), 1)
    tl = jnp.where(lane < R_LORA, jnp.tanh(wa), wa).astype(BF16)
    pre = _dot(tl, w2_ref[...])
    lw = -math.exp(-0.5) * _sigmoid(w0 + pre[:, :MIX_W])
    a = _sigmoid(a0 + pre[:, MIX_W:])

    if has_vres:
        gate = _sigmoid(v0 + _dot(vl_ref[...].astype(BF16), vw2_ref[...]))
        v = v + (vf_ref[...] - v) * gate
    else:
        vfo_ref[...] = v

    ti = lax.broadcasted_iota(jnp.int32, (RWKV_TBLK, RWKV_TBLK), 0)
    si = lax.broadcasted_iota(jnp.int32, (RWKV_TBLK, RWKV_TBLK), 1)
    same_chunk = jnp.right_shift(ti, CHUNK_LOG2) == jnp.right_shift(si, CHUNK_LOG2)
    tri = jnp.where(same_chunk & (si <= ti), 1.0, 0.0).astype(BF16)
    cum = _split_dot(tri, lw, 3)
    ecum = jnp.exp(cum)
    ecum_prev = jnp.exp(cum - lw)
    einv = jnp.exp(-cum)
    chunks = range(RWKV_TBLK // CHUNK)
    rcs = [slice(c * CHUNK, (c + 1) * CHUNK) for c in chunks]
    cls = [cum[(c + 1) * CHUNK - 1:(c + 1) * CHUNK] for c in chunks]
    g_cs = [jnp.exp(cl) for cl in cls]

    groups = range(N_GROUPS)
    gls = [slice(g * GROUP_W, (g + 1) * GROUP_W) for g in groups]
    units = [(c, g) for c in chunks for g in groups]
    t_i = lax.broadcasted_iota(jnp.int32, (CHUNK, GROUP_W), 0)
    l_i = lax.broadcasted_iota(jnp.int32, (CHUNK, GROUP_W), 1)
    s_i = jnp.bitwise_and(l_i, RWKV_HEAD_W - 1)
    strict = t_i > s_i
    incl = t_i >= s_i
    eye = jnp.where(t_i == s_i, 1.0, 0.0).astype(F32)
    head_of_lane = jnp.right_shift(l_i, CHUNK_LOG2)
    head_masks = [jnp.where(head_of_lane == h, 1.0, 0.0).astype(BF16)
                  for h in range(GROUP_W // RWKV_HEAD_W)]
    bi = lax.broadcasted_iota(jnp.int32, (GROUP_W, GROUP_W), 0)
    bj = lax.broadcasted_iota(jnp.int32, (GROUP_W, GROUP_W), 1)
    same_head = jnp.right_shift(bi, CHUNK_LOG2) == jnp.right_shift(bj, CHUNK_LOG2)
    seg = jnp.where(same_head, 1.0, 0.0).astype(BF16)

    def stack4(x):
        xb = x.astype(BF16)
        return jnp.concatenate([xb * m for m in head_masks], axis=0)

    def segsum(x):
        return _dot(x.astype(BF16), seg)

    kk_raw = k * k_k
    k2 = k * (1.0 + (a - 1.0) * k_a)
    sums2 = [segsum(jnp.concatenate([kk_raw[:, gl] * kk_raw[:, gl],
                                     r[:, gl] * k2[:, gl] * r_k[:, gl]], axis=0)) for gl in gls]
    lhss, rhss, bks, v_sts = {}, {}, {}, {}
    for c, g in units:
        rc, gl = rcs[c], gls[g]
        kk = kk_raw[rc, gl] * lax.rsqrt(jnp.maximum(sums2[g][rc], KK_NORM_FLOOR_SQ))
        kka = kk * a[rc, gl]
        at = -kk * ecum_prev[rc, gl]
        bt = kka * einv[rc, gl]
        kt = k2[rc, gl] * einv[rc, gl]
        rt = r[rc, gl] * ecum[rc, gl]
        lhss[c, g] = jnp.concatenate([at, rt], axis=0).astype(BF16)
        rhss[c, g] = jnp.concatenate([stack4(bt), stack4(kt)], axis=0)
        bks[c, g] = jnp.concatenate([bt, kt], axis=0).astype(BF16)
        v_sts[c, g] = stack4(v[rc, gl])

    gs = {u: _dot_nt(lhss[u], rhss[u]) for u in units}
    n_abs = {u: jnp.where(strict, gs[u][0:CHUNK, 0:GROUP_W], 0.0) for u in units}
    g_ks = {u: jnp.concatenate([jnp.where(strict, gs[u][0:CHUNK, GROUP_W:], 0.0),
                                jnp.where(incl, gs[u][CHUNK:, GROUP_W:], 0.0)],
                               axis=0).astype(BF16) for u in units}
    g_rbs = {u: jnp.where(incl, gs[u][CHUNK:, 0:GROUP_W], 0.0).astype(BF16) for u in units}
    gkvs = {u: _dot(g_ks[u], v_sts[u]) for u in units}

    psts = {u: stack4(n_abs[u]) for u in units}
    pws = {u: _dot(n_abs[u].astype(BF16), psts[u]) for u in units}
    sks = {u: eye + n_abs[u] for u in units}
    for _ in range(4):
        psts = {u: stack4(pws[u]) for u in units}
        prods = {u: _dot(jnp.concatenate([sks[u], pws[u]], axis=0).astype(BF16), psts[u])
                 for u in units}
        sks = {u: sks[u] + prods[u][0:CHUNK] for u in units}
        pws = {u: prods[u][CHUNK:] for u in units}
    psts = {u: stack4(pws[u]) for u in units}
    tinvs = {u: (sks[u] + _dot(sks[u].astype(BF16), psts[u])).astype(BF16) for u in units}

    states = [state_ref[g] for g in groups]
    ys = {}
    for c in chunks:
        ahs = [_dot_nt(lhss[c, g], states[g].astype(BF16)) for g in groups]
        us = [_dot(tinvs[c, g], stack4(ahs[g][0:CHUNK] + gkvs[c, g][0:CHUNK])) for g in groups]
        for g in groups:
            ys[c, g] = (ahs[g][CHUNK:] + gkvs[c, g][CHUNK:]
                        + _dot(g_rbs[c, g], stack4(us[g])))
        uvs = [jnp.concatenate([us[g], v[rcs[c], gls[g]]], axis=0).astype(BF16) for g in groups]
        states = [(states[g] + jnp.where(same_head, _dot_tn(uvs[g], bks[c, g]), 0.0))
                  * g_cs[c][:, gls[g]] for g in groups]
    state_ref[...] = jnp.stack(states, axis=0)

    y_rows = [jnp.concatenate([ys[c, g] for c in chunks], axis=0) for g in groups]
    means = [segsum(y) * (1.0 / RWKV_HEAD_W) for y in y_rows]
    ds = [y_rows[g] - means[g] for g in groups]
    variances = [segsum(d * d) * (1.0 / RWKV_HEAD_W) for d in ds]
    outs = []
    for g, gl in enumerate(gls):
        yn = ds[g] * lax.rsqrt(variances[g] + GN_EPS) * lnx_g[:, gl] + lnx_b[:, gl]
        bonus = sums2[g][RWKV_TBLK:] * v[:, gl]
        outs.append((yn + bonus) * _silu(z_ref[:, gl].astype(F32)))
    o_ref[...] = jnp.concatenate(outs, axis=1).astype(o_ref.dtype)


def _rwkv(hm, he, vfirst, pvec, w2, vw2, batch, seq):
    nt = seq // RWKV_TBLK
    has_vres = vfirst is not None
    tok = lambda b, t: b * nt + t
    slab = lambda s: pl.BlockSpec((RWKV_TBLK, MIX_W), lambda b, t: (tok(b, t), s))
    ext = lambda s: pl.BlockSpec((RWKV_TBLK, LANES), lambda b, t: (tok(b, t), s))
    full = lambda a: pl.BlockSpec(a.shape, lambda b, t: (0,) * a.ndim)
    act = pl.BlockSpec((RWKV_TBLK, MIX_W), lambda b, t: (tok(b, t), 0))
    in_specs = [slab(SLAB_R), slab(SLAB_K), slab(SLAB_V), slab(SLAB_ZA), ext(0)]
    args = [hm, hm, hm, hm, he]
    if has_vres:
        in_specs += [ext(1), act, full(pvec), full(w2), full(vw2)]
        args += [he, vfirst, pvec, w2, vw2]
        out_specs = act
        out_shape = jax.ShapeDtypeStruct((batch * seq, MIX_W), BF16)
    else:
        in_specs += [full(pvec), full(w2)]
        args += [pvec, w2]
        out_specs = [act, act]
        out_shape = [jax.ShapeDtypeStruct((batch * seq, MIX_W), BF16),
                     jax.ShapeDtypeStruct((batch * seq, MIX_W), F32)]
    return pl.pallas_call(
        functools.partial(_rwkv_kernel, has_vres),
        grid=(batch, nt),
        in_specs=in_specs,
        out_specs=out_specs,
        out_shape=out_shape,
        scratch_shapes=[
            pltpu.VMEM((N_GROUPS, GROUP_W, GROUP_W), F32),
            pltpu.VMEM((SUBLANES, MIX_W), F32),
            pltpu.VMEM((SUBLANES, MIX_W), F32),
            pltpu.VMEM((SUBLANES, MIX_W), F32),
            pltpu.VMEM((SUBLANES, LANES), F32),
        ],
        compiler_params=pltpu.CompilerParams(
            dimension_semantics=("parallel", "arbitrary"), vmem_limit_bytes=VMEM_LIMIT),
        name="rwkv",
    )(*args)


def _sgu_kernel(u_ref, v_ref, z_ref, lng_ref, lnb_ref, ws_ref, bias_ref, o_ref):
    v = v_ref[...].astype(F32)
    mean = jnp.mean(v, axis=-1, keepdims=True)
    d = v - mean
    var = jnp.mean(d * d, axis=-1, keepdims=True)
    vn = (d * lax.rsqrt(var + EPS) * lng_ref[...] + lnb_ref[...]).astype(BF16)
    ti = lax.broadcasted_iota(jnp.int32, (SGU_BLOCK, SGU_BLOCK), 0)
    si = lax.broadcasted_iota(jnp.int32, (SGU_BLOCK, SGU_BLOCK), 1)
    causal = (ti >= CHUNK) | (si < CHUNK)
    for g in range(SGU_GROUPS):
        sl = slice(g * LANES, (g + 1) * LANES)
        ws = jnp.where(causal, ws_ref[g], 0.0).astype(BF16)
        for b in range(SGU_TBLK // SGU_BLOCK):
            rows = slice(b * SGU_BLOCK, (b + 1) * SGU_BLOCK)
            mixed = _dot(ws, vn[rows, sl]) + bias_ref[:, sl]
            o_ref[rows, sl] = (u_ref[rows, sl].astype(F32) * mixed
                               * _silu(z_ref[rows, sl].astype(F32))).astype(o_ref.dtype)


def _sgu(hm, ln_g, ln_b, ws, bias, batch, seq):
    nb = seq // SGU_TBLK
    slab = lambda s: pl.BlockSpec((SGU_TBLK, MIX_W), lambda b, t: (b * nb + t, s))
    full = lambda a: pl.BlockSpec(a.shape, lambda b, t: (0,) * a.ndim)
    return pl.pallas_call(
        _sgu_kernel,
        grid=(batch, nb),
        in_specs=[slab(SLAB_UB), slab(SLAB_VB), slab(SLAB_ZB),
                  full(ln_g), full(ln_b), full(ws), full(bias)],
        out_specs=pl.BlockSpec((SGU_TBLK, MIX_W), lambda b, t: (b * nb + t, 0)),
        out_shape=jax.ShapeDtypeStruct((batch * seq, MIX_W), BF16),
        compiler_params=pltpu.CompilerParams(
            dimension_semantics=("parallel", "parallel"), vmem_limit_bytes=VMEM_LIMIT),
        name="sgu",
    )(hm, hm, hm, ln_g, ln_b, ws, bias)


def _hgrn_kernel(n_sub, q_ref, f_ref, i_ref, z_ref, lb_ref, g_ref, o_ref, state_ref, kx_ref, fg_ref,
                 ic_ref):
    @pl.when(pl.program_id(2) == 0)
    def _():
        state_ref[...] = jnp.zeros_like(state_ref)

    for sub in range(n_sub):
        _hgrn_block(slice(sub * HGRN_SUB, (sub + 1) * HGRN_SUB), q_ref, f_ref, i_ref, z_ref,
                    lb_ref, g_ref, o_ref, state_ref, kx_ref, fg_ref, ic_ref)


def _hgrn_block(rows, q_ref, f_ref, i_ref, z_ref, lb_ref, g_ref, o_ref, state_ref, kx_ref, fg_ref,
                ic_ref):
    lb = lb_ref[...]
    gi = lax.broadcasted_iota(jnp.int32, (HGRN_SUB, HGRN_SUB), 0)
    gj = lax.broadcasted_iota(jnp.int32, (HGRN_SUB, HGRN_SUB), 1)
    same_step = jnp.right_shift(gi, HGRN_STEP_LOG2) == jnp.right_shift(gj, HGRN_STEP_LOG2)
    tri = jnp.where(same_step & (gj <= gi), 1.0, 0.0).astype(BF16)
    ones = jnp.ones((HGRN_HEAD, HGRN_HEAD), BF16)
    half = HGRN_STEP // 2
    trow = lax.broadcasted_iota(jnp.int32, (half, 1), 0)

    q = _silu(q_ref[rows, :].astype(F32))
    fg = lb + (1.0 - lb) * _sigmoid(f_ref[rows, :].astype(F32))
    kx = 1.0 - fg
    ic = i_ref[rows, :].astype(F32)
    kx_ref[...] = kx
    fg_ref[...] = fg
    ic_ref[...] = ic

    def row_bcast(ref, r):
        return jnp.broadcast_to(ref[r:r + 1, :], (half, HGRN_HEAD))

    bcs = _split_dot(tri, jnp.log(fg), 3)
    steps = range(HGRN_SUB // HGRN_STEP)
    rss = [slice(j * HGRN_STEP, (j + 1) * HGRN_STEP) for j in steps]
    bls = [bcs[rs][HGRN_STEP - 1:HGRN_STEP] for rs in rss]
    atts_lo, atts_hi = [], []
    for j, rs in enumerate(rss):
        qs, base = q[rs], j * HGRN_STEP
        q_lo, q_hi = qs[0:half], qs[half:HGRN_STEP]
        p_lo, p_hi = [None] * half, [None] * HGRN_STEP
        lo = hi = None
        for s in range(HGRN_STEP - 1, -1, -1):
            nxt = row_bcast(fg_ref, base + s + 1) if s + 1 < HGRN_STEP else None
            k_s = row_bcast(kx_ref, base + s)
            if s >= half:
                prev = 0.0 if hi is None else hi * nxt
                hi = jnp.where(trow == s - half, q_hi, prev)
            else:
                hi = hi * nxt
                prev = 0.0 if lo is None else lo * nxt
                lo = jnp.where(trow == s, q_lo, prev)
                p_lo[s] = lo * k_s
            p_hi[s] = hi * k_s
        stack = jnp.concatenate(p_lo + p_hi, axis=0).astype(BF16)
        att = _dot(stack, ones)
        atts_lo.append(att[0:half * half])
        atts_hi.append(att[half * half:])
    incs = [_dot_tn(ic[rs].astype(BF16), (kx[rs] * jnp.exp(bl - bcs[rs])).astype(BF16))
            for rs, bl in zip(rss, bls)]
    qb = (q * jnp.exp(bcs)).astype(BF16)
    st = state_ref[...]
    o_inters = []
    for j in steps:
        o_inters.append(_dot_nt(qb[rss[j]], st.astype(BF16)))
        st = st * jnp.exp(bls[j]) + incs[j]
    state_ref[...] = st
    outs = []
    for j, rs in enumerate(rss):
        a_lo, a_hi, base = atts_lo[j], atts_hi[j], j * HGRN_STEP
        o_lo = a_lo[0:half] * row_bcast(ic_ref, base)
        for s in range(1, half):
            o_lo = o_lo + a_lo[s * half:(s + 1) * half] * row_bcast(ic_ref, base + s)
        o_hi = a_hi[0:half] * row_bcast(ic_ref, base)
        for s in range(1, HGRN_STEP):
            o_hi = o_hi + a_hi[s * half:(s + 1) * half] * row_bcast(ic_ref, base + s)
        outs.append(o_inters[j] + jnp.concatenate([o_lo, o_hi], axis=0))
    o = jnp.concatenate(outs, axis=0)
    o = o * lax.rsqrt(jnp.mean(o * o, axis=-1, keepdims=True) + EPS) * g_ref[...]
    o_ref[rows, :] = (o * _silu(z_ref[rows, :].astype(F32))).astype(o_ref.dtype)


def _hgrn(hm, lb, g_norm, batch, seq):
    tblk = min(HGRN_TBLK, seq)
    nt = seq // tblk
    col = lambda s: pl.BlockSpec(
        (tblk, HGRN_HEAD), lambda b, h, t: (b * nt + t, s * HGRN_HEADS + h))
    vec = pl.BlockSpec((1, HGRN_HEAD), lambda b, h, t: (0, h))
    return pl.pallas_call(
        functools.partial(_hgrn_kernel, tblk // HGRN_SUB),
        grid=(batch, HGRN_HEADS, nt),
        in_specs=[col(SLAB_QC), col(SLAB_FC), col(SLAB_IC), col(SLAB_ZC), vec, vec],
        out_specs=pl.BlockSpec((tblk, HGRN_HEAD), lambda b, h, t: (b * nt + t, h)),
        out_shape=jax.ShapeDtypeStruct((batch * seq, MIX_W), BF16),
        scratch_shapes=[pltpu.VMEM((HGRN_HEAD, HGRN_HEAD), F32),
                        pltpu.VMEM((HGRN_SUB, HGRN_HEAD), F32),
                        pltpu.VMEM((HGRN_SUB, HGRN_HEAD), F32),
                        pltpu.VMEM((HGRN_SUB, HGRN_HEAD), F32)],
        compiler_params=pltpu.CompilerParams(
            dimension_semantics=("parallel", "parallel", "arbitrary"),
            vmem_limit_bytes=VMEM_LIMIT),
        name="hgrn",
    )(hm, hm, hm, hm, lb, g_norm)


def _merge_kernel(final, *refs):
    if final:
        oa, ob, oc, g0, g1, g2, x_ref, bm_ref, wb_ref, wo_ref, fg_ref, out_ref = refs
    else:
        oa, ob, oc, g0, g1, g2, x_ref, bm_ref, wb_ref, wo_ref, out_ref = refs
    y = None
    for n, (o_ref, gl_ref) in enumerate(((oa, g0), (ob, g1), (oc, g2))):
        proj = _dot(o_ref[...], wb_ref[n])
        term = _sigmoid(gl_ref[...].astype(F32) + bm_ref[n:n + 1, :]) * proj
        y = term if y is None else y + term
    xo = x_ref[...] + _dot(y.astype(BF16), wo_ref[...])
    if final:
        ms = jnp.mean(xo * xo, axis=-1, keepdims=True)
        xo = xo * lax.rsqrt(ms + EPS) * fg_ref[...]
    out_ref[...] = xo


def _merge(oa, ob, oc, hm, x2, bm, wb, wo, final_g):
    m = x2.shape[0]
    tm = min(MERGE_TM, m)
    final = final_g is not None
    act = pl.BlockSpec((tm, D_MODEL), lambda i: (i, 0))
    gl = lambda n: pl.BlockSpec((tm, D_MODEL), lambda i: (i, SLAB_GL + n))
    full = lambda a: pl.BlockSpec(a.shape, lambda i: (0,) * a.ndim)
    in_specs = [act, act, act, gl(0), gl(1), gl(2), act, full(bm), full(wb), full(wo)]
    args = [oa, ob, oc, hm, hm, hm, x2, bm, wb, wo]
    if final:
        in_specs.append(full(final_g))
        args.append(final_g)
    return pl.pallas_call(
        functools.partial(_merge_kernel, final),
        grid=(m // tm,),
        in_specs=in_specs,
        out_specs=act,
        out_shape=jax.ShapeDtypeStruct((m, D_MODEL), F32),
        compiler_params=pltpu.CompilerParams(
            dimension_semantics=("parallel",), vmem_limit_bytes=VMEM_LIMIT),
        name="merge",
    )(*args)


def kernel(x, norm_g, w_in, b_merge, mu_shift, w0, w_w2, a0, a_w2, v0, v_w1, v_w2, k_k, k_a, r_k,
           lnx_g, lnx_b, sgu_ln_g, sgu_ln_b, sgu_w, sgu_b, lb_logits, hgrn_g, w_branch, w_out,
           final_g):
    batch, seq, _ = x.shape
    depth = w_in.shape[0]
    assert seq % min(HGRN_TBLK, seq) == 0 and seq % HGRN_SUB == 0 and x.shape[2] == D_MODEL
    assert seq % RWKV_TBLK == 0 and seq % SGU_TBLK == 0
    n_shift = 3 * MIX_W + 2 * R_LORA

    lb_all = jnp.cumsum(jax.nn.softmax(lb_logits.astype(F32), axis=0), axis=0)
    lb_all = lb_all - lb_all[0]

    x2 = x.reshape(batch * seq, D_MODEL)
    vfirst = None
    for l in range(depth):
        w = w_in[l]
        wm = jnp.concatenate([w[:, :3 * MIX_W], w[:, n_shift:]], axis=1).astype(BF16)
        ext = [w[:, 3 * MIX_W:n_shift]]
        if l > 0:
            ext.append(v_w1[l - 1])
        we = jnp.concatenate(ext, axis=1)
        we = jnp.pad(we, ((0, 0), (0, N_EXT - we.shape[1]))).astype(BF16)

        zrow = jnp.zeros((MIX_W,), F32)
        mu = mu_shift[l]
        rows = [mu[0:MIX_W], mu[MIX_W:2 * MIX_W], mu[2 * MIX_W:3 * MIX_W], w0[l], a0[l], k_k[l],
                k_a[l], r_k[l], lnx_g[l], lnx_b[l],
                v0[l - 1] if l > 0 else zrow,
                jnp.pad(mu[3 * MIX_W:n_shift], (0, MIX_W - 2 * R_LORA))]
        assert len(rows) == PV_ROWS
        pvec = jnp.stack(rows + [zrow] * (2 * SUBLANES - PV_ROWS), axis=0)
        zl = jnp.zeros((R_LORA, MIX_W), F32)
        w2 = jnp.concatenate(
            [jnp.concatenate([w_w2[l], zl], axis=0), jnp.concatenate([zl, a_w2[l]], axis=0)],
            axis=1).astype(BF16)

        hm, he = _proj_in(x2, norm_g[l][None, :], wm, we)

        if l == 0:
            oa, vfirst = _rwkv(hm, he, None, pvec, w2, None, batch, seq)
        else:
            vw2 = jnp.pad(v_w2[l - 1], ((0, LANES - R_V), (0, 0))).astype(BF16)
            oa = _rwkv(hm, he, vfirst, pvec, w2, vw2, batch, seq)

        bias = jnp.repeat(sgu_b[l].T, LANES, axis=1)
        ob = _sgu(hm, sgu_ln_g[l][None, :], sgu_ln_b[l][None, :], sgu_w[l], bias, batch, seq)
        oc = _hgrn(hm, lb_all[l][None, :], hgrn_g[l][None, :], batch, seq)

        bm = jnp.pad(b_merge[l], ((0, SUBLANES - N_BRANCH), (0, 0)))
        x2 = _merge(oa, ob, oc, hm, x2, bm, w_branch[l].astype(BF16), w_out[l].astype(BF16),
                    final_g[None, :] if l == depth - 1 else None)
    return x2.reshape(batch, seq, D_MODEL)
```

```python
import functools
import math

import jax
import jax.numpy as jnp
from jax import lax
from jax.experimental import pallas as pl
from jax.experimental.pallas import tpu as pltpu

F32 = jnp.float32
BF16 = jnp.bfloat16

D_MODEL = 1024
MIX_W = 1024
N_BRANCH = 3
CHUNK = 64
CHUNK_LOG2 = 6
RWKV_TBLK = 256
RWKV_HEAD_W = 64
R_LORA = 64
R_V = 32
SGU_BLOCK = 128
SGU_TBLK = 1024
SGU_GROUPS = 8
HGRN_HEAD = 128
HGRN_HEADS = MIX_W // HGRN_HEAD
HGRN_STEP = 16
HGRN_STEP_LOG2 = 4
HGRN_SUB = 512
HGRN_TBLK = 4096
EPS = 1e-6
GN_EPS = 64e-5
LANES = 128
SUBLANES = 8
GROUP_W = 256
N_GROUPS = MIX_W // GROUP_W
N_MAIN = 14 * 1024
N_EXT = 256
VMEM_LIMIT = 48 * 1024 * 1024
PROJ_TM, PROJ_TN = 2048, 1024
MERGE_TM = 512
KK_NORM_FLOOR_SQ = 1e-24
(PV_MU_R, PV_MU_K, PV_MU_V, PV_W0, PV_A0, PV_KK, PV_KA, PV_RK, PV_LNG, PV_LNB, PV_V0,
 PV_MU_WA, PV_ROWS) = range(12 + 1)

SLAB_R, SLAB_K, SLAB_V, SLAB_ZA, SLAB_UB, SLAB_VB, SLAB_ZB = 0, 1, 2, 3, 4, 5, 6
SLAB_QC, SLAB_FC, SLAB_IC, SLAB_ZC, SLAB_GL = 7, 8, 9, 10, 11


def _dot(a, b):
    return jnp.dot(a, b, preferred_element_type=F32)


def _dot_nt(a, b):
    return lax.dot_general(a, b, (((1,), (1,)), ((), ())), preferred_element_type=F32)


def _dot_tn(a, b):
    return lax.dot_general(a, b, (((0,), (0,)), ((), ())), preferred_element_type=F32)


def _split_dot(sel, x, terms):
    acc = None
    rem = x
    for _ in range(terms):
        part = rem.astype(BF16)
        d = _dot(sel, part)
        acc = d if acc is None else acc + d
        rem = rem - part.astype(F32)
    return acc


def _sigmoid(x):
    return jax.nn.sigmoid(x)


def _silu(x):
    return x * jax.nn.sigmoid(x)


def _proj_kernel(x_ref, g_ref, wm_ref, we_ref, hm_ref, he_ref, xn_ref):
    @pl.when(pl.program_id(1) == 0)
    def _():
        x = x_ref[...]
        ms = jnp.mean(x * x, axis=-1, keepdims=True)
        xn = (x * lax.rsqrt(ms + EPS) * g_ref[...]).astype(BF16)
        xn_ref[...] = xn
        he_ref[...] = _dot(xn, we_ref[...])

    hm_ref[...] = _dot(xn_ref[...], wm_ref[...]).astype(hm_ref.dtype)


def _proj_in(x2, g, wm, we):
    m = x2.shape[0]
    tm, tn = min(PROJ_TM, m), PROJ_TN
    return pl.pallas_call(
        _proj_kernel,
        grid=(m // tm, N_MAIN // tn),
        in_specs=[
            pl.BlockSpec((tm, D_MODEL), lambda i, j: (i, 0)),
            pl.BlockSpec((1, D_MODEL), lambda i, j: (0, 0)),
            pl.BlockSpec((D_MODEL, tn), lambda i, j: (0, j)),
            pl.BlockSpec((D_MODEL, N_EXT), lambda i, j: (0, 0)),
        ],
        out_specs=[
            pl.BlockSpec((tm, tn), lambda i, j: (i, j)),
            pl.BlockSpec((tm, N_EXT), lambda i, j: (i, 0)),
        ],
        out_shape=[
            jax.ShapeDtypeStruct((m, N_MAIN), BF16),
            jax.ShapeDtypeStruct((m, N_EXT), F32),
        ],
        scratch_shapes=[pltpu.VMEM((tm, D_MODEL), BF16)],
        compiler_params=pltpu.CompilerParams(
            dimension_semantics=("parallel", "arbitrary"), vmem_limit_bytes=VMEM_LIMIT),
        name="proj_in",
    )(x2, g, wm, we)


def _rwkv_kernel(has_vres, *refs):
    if has_vres:
        (r_ref, k_ref, v_ref, z_ref, wa_ref, vl_ref, vf_ref, pv_ref, w2_ref, vw2_ref,
         o_ref, state_ref, cr_ref, ck_ref, cv_ref, cwa_ref) = refs
    else:
        (r_ref, k_ref, v_ref, z_ref, wa_ref, pv_ref, w2_ref,
         o_ref, vfo_ref, state_ref, cr_ref, ck_ref, cv_ref, cwa_ref) = refs

    @pl.when(pl.program_id(1) == 0)
    def _():
        for ref in (state_ref, cr_ref, ck_ref, cv_ref, cwa_ref):
            ref[...] = jnp.zeros_like(ref)

    row = lax.broadcasted_iota(jnp.int32, (RWKV_TBLK, 1), 0)

    def shift_mix(x_ref, c_ref, mu):
        x = x_ref[...].astype(F32)
        prev = c_ref[SUBLANES - 1:SUBLANES, :]
        xs = jnp.where(row == 0, prev, pltpu.roll(x, 1, 0))
        c_ref[...] = x[RWKV_TBLK - SUBLANES:RWKV_TBLK, :]
        return x + mu * (xs - x)

    pv = pv_ref[...]
    prow = lambda i: pv[i:i + 1]
    mu_r, mu_k, mu_v = prow(PV_MU_R), prow(PV_MU_K), prow(PV_MU_V)
    w0, a0, k_k, k_a, r_k = prow(PV_W0), prow(PV_A0), prow(PV_KK), prow(PV_KA), prow(PV_RK)
    lnx_g, lnx_b, v0 = prow(PV_LNG), prow(PV_LNB), prow(PV_V0)
    mu_wa = prow(PV_MU_WA)[:, 0:LANES]

    r = shift_mix(r_ref, cr_ref, mu_r)
    k = shift_mix(k_ref, ck_ref, mu_k)
    v = shift_mix(v_ref, cv_ref, mu_v)
    wa = shift_mix(wa_ref, cwa_ref, mu_wa)

    lane = lax.broadcasted_iota(jnp.int32, (1, LANES), 1)
    tl = jnp.where(lane < R_LORA, jnp.tanh(wa), wa).astype(BF16)
    pre = _dot(tl, w2_ref[...])
    lw = -math.exp(-0.5) * _sigmoid(w0 + pre[:, :MIX_W])
    a = _sigmoid(a0 + pre[:, MIX_W:])

    if has_vres:
        gate = _sigmoid(v0 + _dot(vl_ref[...].astype(BF16), vw2_ref[...]))
        v = v + (vf_ref[...] - v) * gate
    else:
        vfo_ref[...] = v

    ti = lax.broadcasted_iota(jnp.int32, (RWKV_TBLK, RWKV_TBLK), 0)
    si = lax.broadcasted_iota(jnp.int32, (RWKV_TBLK, RWKV_TBLK), 1)
    same_chunk = jnp.right_shift(ti, CHUNK_LOG2) == jnp.right_shift(si, CHUNK_LOG2)
    tri = jnp.where(same_chunk & (si <= ti), 1.0, 0.0).astype(BF16)
    cum = _split_dot(tri, lw, 2)
    ecum = jnp.exp(cum)
    ecum_prev = jnp.exp(cum - lw)
    einv = jnp.exp(-cum)
    chunks = range(RWKV_TBLK // CHUNK)
    rcs = [slice(c * CHUNK, (c + 1) * CHUNK) for c in chunks]
    cls = [cum[(c + 1) * CHUNK - 1:(c + 1) * CHUNK] for c in chunks]
    g_cs = [jnp.exp(cl) for cl in cls]

    groups = range(N_GROUPS)
    gls = [slice(g * GROUP_W, (g + 1) * GROUP_W) for g in groups]
    units = [(c, g) for c in chunks for g in groups]
    t_i = lax.broadcasted_iota(jnp.int32, (CHUNK, GROUP_W), 0)
    l_i = lax.broadcasted_iota(jnp.int32, (CHUNK, GROUP_W), 1)
    s_i = jnp.bitwise_and(l_i, RWKV_HEAD_W - 1)
    strict = t_i > s_i
    incl = t_i >= s_i
    eye = jnp.where(t_i == s_i, 1.0, 0.0).astype(F32)
    head_of_lane = jnp.right_shift(l_i, CHUNK_LOG2)
    head_masks = [jnp.where(head_of_lane == h, 1.0, 0.0).astype(BF16)
                  for h in range(GROUP_W // RWKV_HEAD_W)]
    bi = lax.broadcasted_iota(jnp.int32, (GROUP_W, GROUP_W), 0)
    bj = lax.broadcasted_iota(jnp.int32, (GROUP_W, GROUP_W), 1)
    same_head = jnp.right_shift(bi, CHUNK_LOG2) == jnp.right_shift(bj, CHUNK_LOG2)
    seg = jnp.where(same_head, 1.0, 0.0).astype(BF16)

    def stack4(x):
        xb = x.astype(BF16)
        return jnp.concatenate([xb * m for m in head_masks], axis=0)

    def segsum(x):
        return _dot(x.astype(BF16), seg)

    kk_raw = k * k_k
    k2 = k * (1.0 + (a - 1.0) * k_a)
    sums2 = [segsum(jnp.concatenate([kk_raw[:, gl] * kk_raw[:, gl],
                                     r[:, gl] * k2[:, gl] * r_k[:, gl]], axis=0)) for gl in gls]
    lhss, rhss, bks, v_sts = {}, {}, {}, {}
    for c, g in units:
        rc, gl = rcs[c], gls[g]
        kk = kk_raw[rc, gl] * lax.rsqrt(jnp.maximum(sums2[g][rc], KK_NORM_FLOOR_SQ))
        kka = kk * a[rc, gl]
        at = -kk * ecum_prev[rc, gl]
        bt = kka * einv[rc, gl]
        kt = k2[rc, gl] * einv[rc, gl]
        rt = r[rc, gl] * ecum[rc, gl]
        lhss[c, g] = jnp.concatenate([at, rt], axis=0).astype(BF16)
        rhss[c, g] = jnp.concatenate([stack4(bt), stack4(kt)], axis=0)
        bks[c, g] = jnp.concatenate([bt, kt], axis=0).astype(BF16)
        v_sts[c, g] = stack4(v[rc, gl])

    gs = {u: _dot_nt(lhss[u], rhss[u]) for u in units}
    n_abs = {u: jnp.where(strict, gs[u][0:CHUNK, 0:GROUP_W], 0.0) for u in units}
    g_ks = {u: jnp.concatenate([jnp.where(strict, gs[u][0:CHUNK, GROUP_W:], 0.0),
                                jnp.where(incl, gs[u][CHUNK:, GROUP_W:], 0.0)],
                               axis=0).astype(BF16) for u in units}
    g_rbs = {u: jnp.where(incl, gs[u][CHUNK:, 0:GROUP_W], 0.0).astype(BF16) for u in units}
    gkvs = {u: _dot(g_ks[u], v_sts[u]) for u in units}

    psts = {u: stack4(n_abs[u]) for u in units}
    pws = {u: _dot(n_abs[u].astype(BF16), psts[u]) for u in units}
    sks = {u: eye + n_abs[u] for u in units}
    for _ in range(4):
        psts = {u: stack4(pws[u]) for u in units}
        prods = {u: _dot(jnp.concatenate([sks[u], pws[u]], axis=0).astype(BF16), psts[u])
                 for u in units}
        sks = {u: sks[u] + prods[u][0:CHUNK] for u in units}
        pws = {u: prods[u][CHUNK:] for u in units}
    psts = {u: stack4(pws[u]) for u in units}
    tinvs = {u: (sks[u] + _dot(sks[u].astype(BF16), psts[u])).astype(BF16) for u in units}

    states = [state_ref[g] for g in groups]
    ys = {}
    for c in chunks:
        ahs = [_dot_nt(lhss[c, g], states[g].astype(BF16)) for g in groups]
        us = [_dot(tinvs[c, g], stack4(ahs[g][0:CHUNK] + gkvs[c, g][0:CHUNK])) for g in groups]
        for g in groups:
            ys[c, g] = (ahs[g][CHUNK:] + gkvs[c, g][CHUNK:]
                        + _dot(g_rbs[c, g], stack4(us[g])))
        uvs = [jnp.concatenate([us[g], v[rcs[c], gls[g]]], axis=0).astype(BF16) for g in groups]
        states = [(states[g] + jnp.where(same_head, _dot_tn(uvs[g], bks[c, g]), 0.0))
                  * g_cs[c][:, gls[g]] for g in groups]
    state_ref[...] = jnp.stack(states, axis=0)

    y_rows = [jnp.concatenate([ys[c, g] for c in chunks], axis=0) for g in groups]
    means = [segsum(y) * (1.0 / RWKV_HEAD_W) for y in y_rows]
    ds = [y_rows[g] - means[g] for g in groups]
    variances = [segsum(d * d) * (1.0 / RWKV_HEAD_W) for d in ds]
    outs = []
    for g, gl in enumerate(gls):
        yn = ds[g] * lax.rsqrt(variances[g] + GN_EPS) * lnx_g[:, gl] + lnx_b[:, gl]
        bonus = sums2[g][RWKV_TBLK:] * v[:, gl]
        outs.append((yn + bonus) * _silu(z_ref[:, gl].astype(F32)))
    o_ref[...] = jnp.concatenate(outs, axis=1).astype(o_ref.dtype)


def _rwkv(hm, he, vfirst, pvec, w2, vw2, batch, seq):
    nt = seq // RWKV_TBLK
    has_vres = vfirst is not None
    tok = lambda b, t: b * nt + t
    slab = lambda s: pl.BlockSpec((RWKV_TBLK, MIX_W), lambda b, t: (tok(b, t), s))
    ext = lambda s: pl.BlockSpec((RWKV_TBLK, LANES), lambda b, t: (tok(b, t), s))
    full = lambda a: pl.BlockSpec(a.shape, lambda b, t: (0,) * a.ndim)
    act = pl.BlockSpec((RWKV_TBLK, MIX_W), lambda b, t: (tok(b, t), 0))
    in_specs = [slab(SLAB_R), slab(SLAB_K), slab(SLAB_V), slab(SLAB_ZA), ext(0)]
    args = [hm, hm, hm, hm, he]
    if has_vres:
        in_specs += [ext(1), act, full(pvec), full(w2), full(vw2)]
        args += [he, vfirst, pvec, w2, vw2]
        out_specs = act
        out_shape = jax.ShapeDtypeStruct((batch * seq, MIX_W), BF16)
    else:
        in_specs += [full(pvec), full(w2)]
        args += [pvec, w2]
        out_specs = [act, act]
        out_shape = [jax.ShapeDtypeStruct((batch * seq, MIX_W), BF16),
                     jax.ShapeDtypeStruct((batch * seq, MIX_W), F32)]
    return pl.pallas_call(
        functools.partial(_rwkv_kernel, has_vres),
        grid=(batch, nt),
        in_specs=in_specs,
        out_specs=out_specs,
        out_shape=out_shape,
        scratch_shapes=[
            pltpu.VMEM((N_GROUPS, GROUP_W, GROUP_W), F32),
            pltpu.VMEM((SUBLANES, MIX_W), F32),
            pltpu.VMEM((SUBLANES, MIX_W), F32),
            pltpu.VMEM((SUBLANES, MIX_W), F32),
            pltpu.VMEM((SUBLANES, LANES), F32),
        ],
        compiler_params=pltpu.CompilerParams(
            dimension_semantics=("parallel", "arbitrary"), vmem_limit_bytes=VMEM_LIMIT),
        name="rwkv",
    )(*args)


def _sgu_kernel(u_ref, v_ref, z_ref, lng_ref, lnb_ref, ws_ref, bias_ref, o_ref):
    v = v_ref[...].astype(F32)
    mean = jnp.mean(v, axis=-1, keepdims=True)
    d = v - mean
    var = jnp.mean(d * d, axis=-1, keepdims=True)
    vn = (d * lax.rsqrt(var + EPS) * lng_ref[...] + lnb_ref[...]).astype(BF16)
    ti = lax.broadcasted_iota(jnp.int32, (SGU_BLOCK, SGU_BLOCK), 0)
    si = lax.broadcasted_iota(jnp.int32, (SGU_BLOCK, SGU_BLOCK), 1)
    causal = (ti >= CHUNK) | (si < CHUNK)
    for g in range(SGU_GROUPS):
        sl = slice(g * LANES, (g + 1) * LANES)
        ws = jnp.where(causal, ws_ref[g], 0.0).astype(BF16)
        for b in range(SGU_TBLK // SGU_BLOCK):
            rows = slice(b * SGU_BLOCK, (b + 1) * SGU_BLOCK)
            mixed = _dot(ws, vn[rows, sl]) + bias_ref[:, sl]
            o_ref[rows, sl] = (u_ref[rows, sl].astype(F32) * mixed
                               * _silu(z_ref[rows, sl].astype(F32))).astype(o_ref.dtype)


def _sgu(hm, ln_g, ln_b, ws, bias, batch, seq):
    nb = seq // SGU_TBLK
    slab = lambda s: pl.BlockSpec((SGU_TBLK, MIX_W), lambda b, t: (b * nb + t, s))
    full = lambda a: pl.BlockSpec(a.shape, lambda b, t: (0,) * a.ndim)
    return pl.pallas_call(
        _sgu_kernel,
        grid=(batch, nb),
        in_specs=[slab(SLAB_UB), slab(SLAB_VB), slab(SLAB_ZB),
                  full(ln_g), full(ln_b), full(ws), full(bias)],
        out_specs=pl.BlockSpec((SGU_TBLK, MIX_W), lambda b, t: (b * nb + t, 0)),
        out_shape=jax.ShapeDtypeStruct((batch * seq, MIX_W), BF16),
        compiler_params=pltpu.CompilerParams(
            dimension_semantics=("parallel", "parallel"), vmem_limit_bytes=VMEM_LIMIT),
        name="sgu",
    )(hm, hm, hm, ln_g, ln_b, ws, bias)


def _hgrn_kernel(n_sub, q_ref, f_ref, i_ref, z_ref, lb_ref, g_ref, o_ref, state_ref, kx_ref, fg_ref,
                 ic_ref):
    @pl.when(pl.program_id(2) == 0)
    def _():
        state_ref[...] = jnp.zeros_like(state_ref)

    for sub in range(n_sub):
        _hgrn_block(slice(sub * HGRN_SUB, (sub + 1) * HGRN_SUB), q_ref, f_ref, i_ref, z_ref,
                    lb_ref, g_ref, o_ref, state_ref, kx_ref, fg_ref, ic_ref)


def _hgrn_block(rows, q_ref, f_ref, i_ref, z_ref, lb_ref, g_ref, o_ref, state_ref, kx_ref, fg_ref,
                ic_ref):
    lb = lb_ref[...]
    gi = lax.broadcasted_iota(jnp.int32, (HGRN_SUB, HGRN_SUB), 0)
    gj = lax.broadcasted_iota(jnp.int32, (HGRN_SUB, HGRN_SUB), 1)
    same_step = jnp.right_shift(gi, HGRN_STEP_LOG2) == jnp.right_shift(gj, HGRN_STEP_LOG2)
    tri = jnp.where(same_step & (gj <= gi), 1.0, 0.0).astype(BF16)
    ones = jnp.ones((HGRN_HEAD, HGRN_HEAD), BF16)
    half = HGRN_STEP // 2
    trow = lax.broadcasted_iota(jnp.int32, (half, 1), 0)

    q = _silu(q_ref[rows, :].astype(F32))
    fg = lb + (1.0 - lb) * _sigmoid(f_ref[rows, :].astype(F32))
    kx = 1.0 - fg
    ic = i_ref[rows, :].astype(F32)
    kx_ref[...] = kx
    fg_ref[...] = fg
    ic_ref[...] = ic

    def row_bcast(ref, r):
        return jnp.broadcast_to(ref[r:r + 1, :], (half, HGRN_HEAD))

    bcs = _split_dot(tri, jnp.log(fg), 2)
    steps = range(HGRN_SUB // HGRN_STEP)
    rss = [slice(j * HGRN_STEP, (j + 1) * HGRN_STEP) for j in steps]
    bls = [bcs[rs][HGRN_STEP - 1:HGRN_STEP] for rs in rss]
    atts_lo, atts_hi = [], []
    for j, rs in enumerate(rss):
        qs, base = q[rs], j * HGRN_STEP
        q_lo, q_hi = qs[0:half], qs[half:HGRN_STEP]
        p_lo, p_hi = [None] * half, [None] * HGRN_STEP
        lo = hi = None
        for s in range(HGRN_STEP - 1, -1, -1):
            nxt = row_bcast(fg_ref, base + s + 1) if s + 1 < HGRN_STEP else None
            k_s = row_bcast(kx_ref, base + s)
            if s >= half:
                prev = 0.0 if hi is None else hi * nxt
                hi = jnp.where(trow == s - half, q_hi, prev)
            else:
                hi = hi * nxt
                prev = 0.0 if lo is None else lo * nxt
                lo = jnp.where(trow == s, q_lo, prev)
                p_lo[s] = lo * k_s
            p_hi[s] = hi * k_s
        stack = jnp.concatenate(p_lo + p_hi, axis=0).astype(BF16)
        att = _dot(stack, ones)
        atts_lo.append(att[0:half * half])
        atts_hi.append(att[half * half:])
    incs = [_dot_tn(ic[rs].astype(BF16), (kx[rs] * jnp.exp(bl - bcs[rs])).astype(BF16))
            for rs, bl in zip(rss, bls)]
    qb = (q * jnp.exp(bcs)).astype(BF16)
    st = state_ref[...]
    o_inters = []
    for j in steps:
        o_inters.append(_dot_nt(qb[rss[j]], st.astype(BF16)))
        st = st * jnp.exp(bls[j]) + incs[j]
    state_ref[...] = st
    outs = []
    for j, rs in enumerate(rss):
        a_lo, a_hi, base = atts_lo[j], atts_hi[j], j * HGRN_STEP
        o_lo = a_lo[0:half] * row_bcast(ic_ref, base)
        for s in range(1, half):
            o_lo = o_lo + a_lo[s * half:(s + 1) * half] * row_bcast(ic_ref, base + s)
        o_hi = a_hi[0:half] * row_bcast(ic_ref, base)
        for s in range(1, HGRN_STEP):
            o_hi = o_hi + a_hi[s * half:(s + 1) * half] * row_bcast(ic_ref, base + s)
        outs.append(o_inters[j] + jnp.concatenate([o_lo, o_hi], axis=0))
    o = jnp.concatenate(outs, axis=0)
    o = o * lax.rsqrt(jnp.mean(o * o, axis=-1, keepdims=True) + EPS) * g_ref[...]
    o_ref[rows, :] = (o * _silu(z_ref[rows, :].astype(F32))).astype(o_ref.dtype)


def _hgrn(hm, lb, g_norm, batch, seq):
    tblk = min(HGRN_TBLK, seq)
    nt = seq // tblk
    col = lambda s: pl.BlockSpec(
        (tblk, HGRN_HEAD), lambda b, h, t: (b * nt + t, s * HGRN_HEADS + h))
    vec = pl.BlockSpec((1, HGRN_HEAD), lambda b, h, t: (0, h))
    return pl.pallas_call(
        functools.partial(_hgrn_kernel, tblk // HGRN_SUB),
        grid=(batch, HGRN_HEADS, nt),
        in_specs=[col(SLAB_QC), col(SLAB_FC), col(SLAB_IC), col(SLAB_ZC), vec, vec],
        out_specs=pl.BlockSpec((tblk, HGRN_HEAD), lambda b, h, t: (b * nt + t, h)),
        out_shape=jax.ShapeDtypeStruct((batch * seq, MIX_W), BF16),
        scratch_shapes=[pltpu.VMEM((HGRN_HEAD, HGRN_HEAD), F32),
                        pltpu.VMEM((HGRN_SUB, HGRN_HEAD), F32),
                        pltpu.VMEM((HGRN_SUB, HGRN_HEAD), F32),
                        pltpu.VMEM((HGRN_SUB, HGRN_HEAD), F32)],
        compiler_params=pltpu.CompilerParams(
            dimension_semantics=("parallel", "parallel", "arbitrary"),
            vmem_limit_bytes=VMEM_LIMIT),
        name="hgrn",
    )(hm, hm, hm, hm, lb, g_norm)


def _merge_kernel(final, *refs):
    if final:
        oa, ob, oc, g0, g1, g2, x_ref, bm_ref, wb_ref, wo_ref, fg_ref, out_ref = refs
    else:
        oa, ob, oc, g0, g1, g2, x_ref, bm_ref, wb_ref, wo_ref, out_ref = refs
    y = None
    for n, (o_ref, gl_ref) in enumerate(((oa, g0), (ob, g1), (oc, g2))):
        proj = _dot(o_ref[...], wb_ref[n])
        term = _sigmoid(gl_ref[...].astype(F32) + bm_ref[n:n + 1, :]) * proj
        y = term if y is None else y + term
    xo = x_ref[...] + _dot(y.astype(BF16), wo_ref[...])
    if final:
        ms = jnp.mean(xo * xo, axis=-1, keepdims=True)
        xo = xo * lax.rsqrt(ms + EPS) * fg_ref[...]
    out_ref[...] = xo


def _merge(oa, ob, oc, hm, x2, bm, wb, wo, final_g):
    m = x2.shape[0]
    tm = min(MERGE_TM, m)
    final = final_g is not None
    act = pl.BlockSpec((tm, D_MODEL), lambda i: (i, 0))
    gl = lambda n: pl.BlockSpec((tm, D_MODEL), lambda i: (i, SLAB_GL + n))
    full = lambda a: pl.BlockSpec(a.shape, lambda i: (0,) * a.ndim)
    in_specs = [act, act, act, gl(0), gl(1), gl(2), act, full(bm), full(wb), full(wo)]
    args = [oa, ob, oc, hm, hm, hm, x2, bm, wb, wo]
    if final:
        in_specs.append(full(final_g))
        args.append(final_g)
    return pl.pallas_call(
        functools.partial(_merge_kernel, final),
        grid=(m // tm,),
        in_specs=in_specs,
        out_specs=act,
        out_shape=jax.ShapeDtypeStruct((m, D_MODEL), F32),
        compiler_params=pltpu.CompilerParams(
            dimension_semantics=("parallel",), vmem_limit_bytes=VMEM_LIMIT),
        name="merge",
    )(*args)


def kernel(x, norm_g, w_in, b_merge, mu_shift, w0, w_w2, a0, a_w2, v0, v_w1, v_w2, k_k, k_a, r_k,
           lnx_g, lnx_b, sgu_ln_g, sgu_ln_b, sgu_w, sgu_b, lb_logits, hgrn_g, w_branch, w_out,
           final_g):
    batch, seq, _ = x.shape
    depth = w_in.shape[0]
    assert seq % min(HGRN_TBLK, seq) == 0 and seq % HGRN_SUB == 0 and x.shape[2] == D_MODEL
    assert seq % RWKV_TBLK == 0 and seq % SGU_TBLK == 0
    n_shift = 3 * MIX_W + 2 * R_LORA

    lb_all = jnp.cumsum(jax.nn.softmax(lb_logits.astype(F32), axis=0), axis=0)
    lb_all = lb_all - lb_all[0]

    x2 = x.reshape(batch * seq, D_MODEL)
    vfirst = None
    for l in range(depth):
        w = w_in[l]
        wm = jnp.concatenate([w[:, :3 * MIX_W], w[:, n_shift:]], axis=1).astype(BF16)
        ext = [w[:, 3 * MIX_W:n_shift]]
        if l > 0:
            ext.append(v_w1[l - 1])
        we = jnp.concatenate(ext, axis=1)
        we = jnp.pad(we, ((0, 0), (0, N_EXT - we.shape[1]))).astype(BF16)

        zrow = jnp.zeros((MIX_W,), F32)
        mu = mu_shift[l]
        rows = [mu[0:MIX_W], mu[MIX_W:2 * MIX_W], mu[2 * MIX_W:3 * MIX_W], w0[l], a0[l], k_k[l],
                k_a[l], r_k[l], lnx_g[l], lnx_b[l],
                v0[l - 1] if l > 0 else zrow,
                jnp.pad(mu[3 * MIX_W:n_shift], (0, MIX_W - 2 * R_LORA))]
        assert len(rows) == PV_ROWS
        pvec = jnp.stack(rows + [zrow] * (2 * SUBLANES - PV_ROWS), axis=0)
        zl = jnp.zeros((R_LORA, MIX_W), F32)
        w2 = jnp.concatenate(
            [jnp.concatenate([w_w2[l], zl], axis=0), jnp.concatenate([zl, a_w2[l]], axis=0)],
            axis=1).astype(BF16)

        hm, he = _proj_in(x2, norm_g[l][None, :], wm, we)

        if l == 0:
            oa, vfirst = _rwkv(hm, he, None, pvec, w2, None, batch, seq)
        else:
            vw2 = jnp.pad(v_w2[l - 1], ((0, LANES - R_V), (0, 0))).astype(BF16)
            oa = _rwkv(hm, he, vfirst, pvec, w2, vw2, batch, seq)

        bias = jnp.repeat(sgu_b[l].T, LANES, axis=1)
        ob = _sgu(hm, sgu_ln_g[l][None, :], sgu_ln_b[l][None, :], sgu_w[l], bias, batch, seq)
        oc = _hgrn(hm, lb_all[l][None, :], hgrn_g[l][None, :], batch, seq)

        bm = jnp.pad(b_merge[l], ((0, SUBLANES - N_BRANCH), (0, 0)))
        x2 = _merge(oa, ob, oc, hm, x2, bm, w_branch[l].astype(BF16), w_out[l].astype(BF16),
                    final_g[None, :] if l == depth - 1 else None)
    return x2.reshape(batch, seq, D_MODEL)
```

```python
import functools
import math

import jax
import jax.numpy as jnp
from jax import lax
from jax.experimental import pallas as pl
from jax.experimental.pallas import tpu as pltpu

F32 = jnp.float32
BF16 = jnp.bfloat16

D_MODEL = 1024
MIX_W = 1024
N_BRANCH = 3
CHUNK = 64
CHUNK_LOG2 = 6
RWKV_TBLK = 256
RWKV_HEAD_W = 64
R_LORA = 64
R_V = 32
SGU_BLOCK = 128
SGU_GROUPS = 8
HGRN_HEAD = 128
HGRN_HEADS = MIX_W // HGRN_HEAD
HGRN_STEP = 16
HGRN_STEP_LOG2 = 4
HGRN_SUB = 512
HGRN_TBLK = 4096
EPS = 1e-6
GN_EPS = 64e-5
LANES = 128
SUBLANES = 8
GROUP_W = 256
N_GROUPS = MIX_W // GROUP_W
N_MAIN = 14 * 1024
N_EXT = 256
VMEM_LIMIT = 48 * 1024 * 1024
PROJ_TM, PROJ_TN = 2048, 1024
MERGE_TM = 512
KK_NORM_FLOOR_SQ = 1e-24
(PV_MU_R, PV_MU_K, PV_MU_V, PV_W0, PV_A0, PV_KK, PV_KA, PV_RK, PV_LNG, PV_LNB, PV_V0,
 PV_MU_WA, PV_ROWS) = range(12 + 1)

SLAB_R, SLAB_K, SLAB_V, SLAB_ZA, SLAB_UB, SLAB_VB, SLAB_ZB = 0, 1, 2, 3, 4, 5, 6
SLAB_QC, SLAB_FC, SLAB_IC, SLAB_ZC, SLAB_GL = 7, 8, 9, 10, 11


def _dot(a, b):
    return jnp.dot(a, b, preferred_element_type=F32)


def _dot_nt(a, b):
    return lax.dot_general(a, b, (((1,), (1,)), ((), ())), preferred_element_type=F32)


def _dot_tn(a, b):
    return lax.dot_general(a, b, (((0,), (0,)), ((), ())), preferred_element_type=F32)


def _split_dot(sel, x, terms):
    acc = None
    rem = x
    for _ in range(terms):
        part = rem.astype(BF16)
        d = _dot(sel, part)
        acc = d if acc is None else acc + d
        rem = rem - part.astype(F32)
    return acc


def _sigmoid(x):
    return jax.nn.sigmoid(x)


def _silu(x):
    return x * jax.nn.sigmoid(x)


def _proj_kernel(x_ref, g_ref, wm_ref, we_ref, hm_ref, he_ref, xn_ref):
    @pl.when(pl.program_id(1) == 0)
    def _():
        x = x_ref[...]
        ms = jnp.mean(x * x, axis=-1, keepdims=True)
        xn = (x * lax.rsqrt(ms + EPS) * g_ref[...]).astype(BF16)
        xn_ref[...] = xn
        he_ref[...] = _dot(xn, we_ref[...])

    hm_ref[...] = _dot(xn_ref[...], wm_ref[...]).astype(hm_ref.dtype)


def _proj_in(x2, g, wm, we):
    m = x2.shape[0]
    tm, tn = min(PROJ_TM, m), PROJ_TN
    return pl.pallas_call(
        _proj_kernel,
        grid=(m // tm, N_MAIN // tn),
        in_specs=[
            pl.BlockSpec((tm, D_MODEL), lambda i, j: (i, 0)),
            pl.BlockSpec((1, D_MODEL), lambda i, j: (0, 0)),
            pl.BlockSpec((D_MODEL, tn), lambda i, j: (0, j)),
            pl.BlockSpec((D_MODEL, N_EXT), lambda i, j: (0, 0)),
        ],
        out_specs=[
            pl.BlockSpec((tm, tn), lambda i, j: (i, j)),
            pl.BlockSpec((tm, N_EXT), lambda i, j: (i, 0)),
        ],
        out_shape=[
            jax.ShapeDtypeStruct((m, N_MAIN), BF16),
            jax.ShapeDtypeStruct((m, N_EXT), F32),
        ],
        scratch_shapes=[pltpu.VMEM((tm, D_MODEL), BF16)],
        compiler_params=pltpu.CompilerParams(
            dimension_semantics=("parallel", "arbitrary"), vmem_limit_bytes=VMEM_LIMIT),
        name="proj_in",
    )(x2, g, wm, we)


def _rwkv_kernel(has_vres, *refs):
    if has_vres:
        (r_ref, k_ref, v_ref, z_ref, wa_ref, vl_ref, vf_ref, pv_ref, w2_ref, vw2_ref,
         o_ref, state_ref, cr_ref, ck_ref, cv_ref, cwa_ref) = refs
    else:
        (r_ref, k_ref, v_ref, z_ref, wa_ref, pv_ref, w2_ref,
         o_ref, vfo_ref, state_ref, cr_ref, ck_ref, cv_ref, cwa_ref) = refs

    @pl.when(pl.program_id(1) == 0)
    def _():
        for ref in (state_ref, cr_ref, ck_ref, cv_ref, cwa_ref):
            ref[...] = jnp.zeros_like(ref)

    row = lax.broadcasted_iota(jnp.int32, (RWKV_TBLK, 1), 0)

    def shift_mix(x_ref, c_ref, mu):
        x = x_ref[...].astype(F32)
        prev = c_ref[SUBLANES - 1:SUBLANES, :]
        xs = jnp.where(row == 0, prev, pltpu.roll(x, 1, 0))
        c_ref[...] = x[RWKV_TBLK - SUBLANES:RWKV_TBLK, :]
        return x + mu * (xs - x)

    pv = pv_ref[...]
    prow = lambda i: pv[i:i + 1]
    mu_r, mu_k, mu_v = prow(PV_MU_R), prow(PV_MU_K), prow(PV_MU_V)
    w0, a0, k_k, k_a, r_k = prow(PV_W0), prow(PV_A0), prow(PV_KK), prow(PV_KA), prow(PV_RK)
    lnx_g, lnx_b, v0 = prow(PV_LNG), prow(PV_LNB), prow(PV_V0)
    mu_wa = prow(PV_MU_WA)[:, 0:LANES]

    r = shift_mix(r_ref, cr_ref, mu_r)
    k = shift_mix(k_ref, ck_ref, mu_k)
    v = shift_mix(v_ref, cv_ref, mu_v)
    wa = shift_mix(wa_ref, cwa_ref, mu_wa)

    lane = lax.broadcasted_iota(jnp.int32, (1, LANES), 1)
    tl = jnp.where(lane < R_LORA, jnp.tanh(wa), wa).astype(BF16)
    pre = _dot(tl, w2_ref[...])
    lw = -math.exp(-0.5) * _sigmoid(w0 + pre[:, :MIX_W])
    a = _sigmoid(a0 + pre[:, MIX_W:])

    if has_vres:
        gate = _sigmoid(v0 + _dot(vl_ref[...].astype(BF16), vw2_ref[...]))
        v = v + (vf_ref[...] - v) * gate
    else:
        vfo_ref[...] = v

    ti = lax.broadcasted_iota(jnp.int32, (RWKV_TBLK, RWKV_TBLK), 0)
    si = lax.broadcasted_iota(jnp.int32, (RWKV_TBLK, RWKV_TBLK), 1)
    same_chunk = jnp.right_shift(ti, CHUNK_LOG2) == jnp.right_shift(si, CHUNK_LOG2)
    tri = jnp.where(same_chunk & (si <= ti), 1.0, 0.0).astype(BF16)
    cum = _split_dot(tri, lw, 2)
    ecum = jnp.exp(cum)
    ecum_prev = jnp.exp(cum - lw)
    einv = jnp.exp(-cum)
    chunks = range(RWKV_TBLK // CHUNK)
    rcs = [slice(c * CHUNK, (c + 1) * CHUNK) for c in chunks]
    cls = [cum[(c + 1) * CHUNK - 1:(c + 1) * CHUNK] for c in chunks]
    g_cs = [jnp.exp(cl) for cl in cls]

    groups = range(N_GROUPS)
    gls = [slice(g * GROUP_W, (g + 1) * GROUP_W) for g in groups]
    units = [(c, g) for c in chunks for g in groups]
    t_i = lax.broadcasted_iota(jnp.int32, (CHUNK, GROUP_W), 0)
    l_i = lax.broadcasted_iota(jnp.int32, (CHUNK, GROUP_W), 1)
    s_i = jnp.bitwise_and(l_i, RWKV_HEAD_W - 1)
    strict = t_i > s_i
    incl = t_i >= s_i
    eye = jnp.where(t_i == s_i, 1.0, 0.0).astype(F32)
    head_of_lane = jnp.right_shift(l_i, CHUNK_LOG2)
    head_masks = [jnp.where(head_of_lane == h, 1.0, 0.0).astype(BF16)
                  for h in range(GROUP_W // RWKV_HEAD_W)]
    bi = lax.broadcasted_iota(jnp.int32, (GROUP_W, GROUP_W), 0)
    bj = lax.broadcasted_iota(jnp.int32, (GROUP_W, GROUP_W), 1)
    same_head = jnp.right_shift(bi, CHUNK_LOG2) == jnp.right_shift(bj, CHUNK_LOG2)
    seg = jnp.where(same_head, 1.0, 0.0).astype(BF16)

    def stack4(x):
        xb = x.astype(BF16)
        return jnp.concatenate([xb * m for m in head_masks], axis=0)

    def segsum(x):
        return _dot(x.astype(BF16), seg)

    kk_raw = k * k_k
    k2 = k * (1.0 + (a - 1.0) * k_a)
    sums2 = [segsum(jnp.concatenate([kk_raw[:, gl] * kk_raw[:, gl],
                                     r[:, gl] * k2[:, gl] * r_k[:, gl]], axis=0)) for gl in gls]
    lhss, rhss, bks, v_sts = {}, {}, {}, {}
    for c, g in units:
        rc, gl = rcs[c], gls[g]
        kk = kk_raw[rc, gl] * lax.rsqrt(jnp.maximum(sums2[g][rc], KK_NORM_FLOOR_SQ))
        kka = kk * a[rc, gl]
        at = -kk * ecum_prev[rc, gl]
        bt = kka * einv[rc, gl]
        kt = k2[rc, gl] * einv[rc, gl]
        rt = r[rc, gl] * ecum[rc, gl]
        lhss[c, g] = jnp.concatenate([at, rt], axis=0).astype(BF16)
        rhss[c, g] = jnp.concatenate([stack4(bt), stack4(kt)], axis=0)
        bks[c, g] = jnp.concatenate([bt, kt], axis=0).astype(BF16)
        v_sts[c, g] = stack4(v[rc, gl])

    gs = {u: _dot_nt(lhss[u], rhss[u]) for u in units}
    n_abs = {u: jnp.where(strict, gs[u][0:CHUNK, 0:GROUP_W], 0.0) for u in units}
    g_ks = {u: jnp.concatenate([jnp.where(strict, gs[u][0:CHUNK, GROUP_W:], 0.0),
                                jnp.where(incl, gs[u][CHUNK:, GROUP_W:], 0.0)],
                               axis=0).astype(BF16) for u in units}
    g_rbs = {u: jnp.where(incl, gs[u][CHUNK:, 0:GROUP_W], 0.0).astype(BF16) for u in units}
    gkvs = {u: _dot(g_ks[u], v_sts[u]) for u in units}

    psts = {u: stack4(n_abs[u]) for u in units}
    pws = {u: _dot(n_abs[u].astype(BF16), psts[u]) for u in units}
    sks = {u: eye + n_abs[u] for u in units}
    for _ in range(4):
        psts = {u: stack4(pws[u]) for u in units}
        prods = {u: _dot(jnp.concatenate([sks[u], pws[u]], axis=0).astype(BF16), psts[u])
                 for u in units}
        sks = {u: sks[u] + prods[u][0:CHUNK] for u in units}
        pws = {u: prods[u][CHUNK:] for u in units}
    psts = {u: stack4(pws[u]) for u in units}
    tinvs = {u: (sks[u] + _dot(sks[u].astype(BF16), psts[u])).astype(BF16) for u in units}

    states = [state_ref[g] for g in groups]
    ys = {}
    for c in chunks:
        ahs = [_dot_nt(lhss[c, g], states[g].astype(BF16)) for g in groups]
        us = [_dot(tinvs[c, g], stack4(ahs[g][0:CHUNK] + gkvs[c, g][0:CHUNK])) for g in groups]
        for g in groups:
            ys[c, g] = (ahs[g][CHUNK:] + gkvs[c, g][CHUNK:]
                        + _dot(g_rbs[c, g], stack4(us[g])))
        uvs = [jnp.concatenate([us[g], v[rcs[c], gls[g]]], axis=0).astype(BF16) for g in groups]
        states = [(states[g] + jnp.where(same_head, _dot_tn(uvs[g], bks[c, g]), 0.0))
                  * g_cs[c][:, gls[g]] for g in groups]
    state_ref[...] = jnp.stack(states, axis=0)

    y_rows = [jnp.concatenate([ys[c, g] for c in chunks], axis=0) for g in groups]
    means = [segsum(y) * (1.0 / RWKV_HEAD_W) for y in y_rows]
    ds = [y_rows[g] - means[g] for g in groups]
    variances = [segsum(d * d) * (1.0 / RWKV_HEAD_W) for d in ds]
    outs = []
    for g, gl in enumerate(gls):
        yn = ds[g] * lax.rsqrt(variances[g] + GN_EPS) * lnx_g[:, gl] + lnx_b[:, gl]
        bonus = sums2[g][RWKV_TBLK:] * v[:, gl]
        outs.append((yn + bonus) * _silu(z_ref[:, gl].astype(F32)))
    o_ref[...] = jnp.concatenate(outs, axis=1).astype(o_ref.dtype)


def _rwkv(hm, he, vfirst, pvec, w2, vw2, batch, seq):
    nt = seq // RWKV_TBLK
    has_vres = vfirst is not None
    tok = lambda b, t: b * nt + t
    slab = lambda s: pl.BlockSpec((RWKV_TBLK, MIX_W), lambda b, t: (tok(b, t), s))
    ext = lambda s: pl.BlockSpec((RWKV_TBLK, LANES), lambda b, t: (tok(b, t), s))
    full = lambda a: pl.BlockSpec(a.shape, lambda b, t: (0,) * a.ndim)
    act = pl.BlockSpec((RWKV_TBLK, MIX_W), lambda b, t: (tok(b, t), 0))
    in_specs = [slab(SLAB_R), slab(SLAB_K), slab(SLAB_V), slab(SLAB_ZA), ext(0)]
    args = [hm, hm, hm, hm, he]
    if has_vres:
        in_specs += [ext(1), act, full(pvec), full(w2), full(vw2)]
        args += [he, vfirst, pvec, w2, vw2]
        out_specs = act
        out_shape = jax.ShapeDtypeStruct((batch * seq, MIX_W), BF16)
    else:
        in_specs += [full(pvec), full(w2)]
        args += [pvec, w2]
        out_specs = [act, act]
        out_shape = [jax.ShapeDtypeStruct((batch * seq, MIX_W), BF16),
                     jax.ShapeDtypeStruct((batch * seq, MIX_W), F32)]
    return pl.pallas_call(
        functools.partial(_rwkv_kernel, has_vres),
        grid=(batch, nt),
        in_specs=in_specs,
        out_specs=out_specs,
        out_shape=out_shape,
        scratch_shapes=[
            pltpu.VMEM((N_GROUPS, GROUP_W, GROUP_W), F32),
            pltpu.VMEM((SUBLANES, MIX_W), F32),
            pltpu.VMEM((SUBLANES, MIX_W), F32),
            pltpu.VMEM((SUBLANES, MIX_W), F32),
            pltpu.VMEM((SUBLANES, LANES), F32),
        ],
        compiler_params=pltpu.CompilerParams(
            dimension_semantics=("parallel", "arbitrary"), vmem_limit_bytes=VMEM_LIMIT),
        name="rwkv",
    )(*args)


def _sgu_rows(n_rows, u_ref, v_ref, z_ref, lng_ref, lnb_ref, ws_ref, bias_ref):
    v = v_ref[...].astype(F32)
    mean = jnp.mean(v, axis=-1, keepdims=True)
    d = v - mean
    var = jnp.mean(d * d, axis=-1, keepdims=True)
    vn = (d * lax.rsqrt(var + EPS) * lng_ref[...] + lnb_ref[...]).astype(BF16)
    ti = lax.broadcasted_iota(jnp.int32, (SGU_BLOCK, SGU_BLOCK), 0)
    si = lax.broadcasted_iota(jnp.int32, (SGU_BLOCK, SGU_BLOCK), 1)
    causal = (ti >= CHUNK) | (si < CHUNK)
    cols = []
    for g in range(SGU_GROUPS):
        sl = slice(g * LANES, (g + 1) * LANES)
        ws = jnp.where(causal, ws_ref[g], 0.0).astype(BF16)
        blocks = []
        for b in range(n_rows // SGU_BLOCK):
            rows = slice(b * SGU_BLOCK, (b + 1) * SGU_BLOCK)
            mixed = _dot(ws, vn[rows, sl]) + bias_ref[:, sl]
            blocks.append(u_ref[rows, sl].astype(F32) * mixed * _silu(z_ref[rows, sl].astype(F32)))
        cols.append(jnp.concatenate(blocks, axis=0))
    return jnp.concatenate(cols, axis=1)


def _hgrn_kernel(n_sub, q_ref, f_ref, i_ref, z_ref, lb_ref, g_ref, o_ref, state_ref, kx_ref, fg_ref,
                 ic_ref):
    @pl.when(pl.program_id(2) == 0)
    def _():
        state_ref[...] = jnp.zeros_like(state_ref)

    for sub in range(n_sub):
        _hgrn_block(slice(sub * HGRN_SUB, (sub + 1) * HGRN_SUB), q_ref, f_ref, i_ref, z_ref,
                    lb_ref, g_ref, o_ref, state_ref, kx_ref, fg_ref, ic_ref)


def _hgrn_block(rows, q_ref, f_ref, i_ref, z_ref, lb_ref, g_ref, o_ref, state_ref, kx_ref, fg_ref,
                ic_ref):
    lb = lb_ref[...]
    gi = lax.broadcasted_iota(jnp.int32, (HGRN_SUB, HGRN_SUB), 0)
    gj = lax.broadcasted_iota(jnp.int32, (HGRN_SUB, HGRN_SUB), 1)
    same_step = jnp.right_shift(gi, HGRN_STEP_LOG2) == jnp.right_shift(gj, HGRN_STEP_LOG2)
    tri = jnp.where(same_step & (gj <= gi), 1.0, 0.0).astype(BF16)
    ones = jnp.ones((HGRN_HEAD, HGRN_HEAD), BF16)
    half = HGRN_STEP // 2
    trow = lax.broadcasted_iota(jnp.int32, (half, 1), 0)

    q = _silu(q_ref[rows, :].astype(F32))
    fg = lb + (1.0 - lb) * _sigmoid(f_ref[rows, :].astype(F32))
    kx = 1.0 - fg
    ic = i_ref[rows, :].astype(F32)
    kx_ref[...] = kx
    fg_ref[...] = fg
    ic_ref[...] = ic

    def row_bcast(ref, r):
        return jnp.broadcast_to(ref[r:r + 1, :], (half, HGRN_HEAD))

    bcs = _split_dot(tri, jnp.log(fg), 2)
    steps = range(HGRN_SUB // HGRN_STEP)
    rss = [slice(j * HGRN_STEP, (j + 1) * HGRN_STEP) for j in steps]
    bls = [bcs[rs][HGRN_STEP - 1:HGRN_STEP] for rs in rss]
    atts_lo, atts_hi = [], []
    for j, rs in enumerate(rss):
        qs, base = q[rs], j * HGRN_STEP
        q_lo, q_hi = qs[0:half], qs[half:HGRN_STEP]
        p_lo, p_hi = [None] * half, [None] * HGRN_STEP
        lo = hi = None
        for s in range(HGRN_STEP - 1, -1, -1):
            nxt = row_bcast(fg_ref, base + s + 1) if s + 1 < HGRN_STEP else None
            k_s = row_bcast(kx_ref, base + s)
            if s >= half:
                prev = 0.0 if hi is None else hi * nxt
                hi = jnp.where(trow == s - half, q_hi, prev)
            else:
                hi = hi * nxt
                prev = 0.0 if lo is None else lo * nxt
                lo = jnp.where(trow == s, q_lo, prev)
                p_lo[s] = lo * k_s
            p_hi[s] = hi * k_s
        stack = jnp.concatenate(p_lo + p_hi, axis=0).astype(BF16)
        att = _dot(stack, ones)
        atts_lo.append(att[0:half * half])
        atts_hi.append(att[half * half:])
    incs = [_dot_tn(ic[rs].astype(BF16), (kx[rs] * jnp.exp(bl - bcs[rs])).astype(BF16))
            for rs, bl in zip(rss, bls)]
    qb = (q * jnp.exp(bcs)).astype(BF16)
    st = state_ref[...]
    o_inters = []
    for j in steps:
        o_inters.append(_dot_nt(qb[rss[j]], st.astype(BF16)))
        st = st * jnp.exp(bls[j]) + incs[j]
    state_ref[...] = st
    outs = []
    for j, rs in enumerate(rss):
        a_lo, a_hi, base = atts_lo[j], atts_hi[j], j * HGRN_STEP
        o_lo = a_lo[0:half] * row_bcast(ic_ref, base)
        for s in range(1, half):
            o_lo = o_lo + a_lo[s * half:(s + 1) * half] * row_bcast(ic_ref, base + s)
        o_hi = a_hi[0:half] * row_bcast(ic_ref, base)
        for s in range(1, HGRN_STEP):
            o_hi = o_hi + a_hi[s * half:(s + 1) * half] * row_bcast(ic_ref, base + s)
        outs.append(o_inters[j] + jnp.concatenate([o_lo, o_hi], axis=0))
    o = jnp.concatenate(outs, axis=0)
    o = o * lax.rsqrt(jnp.mean(o * o, axis=-1, keepdims=True) + EPS) * g_ref[...]
    o_ref[rows, :] = (o * _silu(z_ref[rows, :].astype(F32))).astype(o_ref.dtype)


def _hgrn(hm, lb, g_norm, batch, seq):
    tblk = min(HGRN_TBLK, seq)
    nt = seq // tblk
    col = lambda s: pl.BlockSpec(
        (tblk, HGRN_HEAD), lambda b, h, t: (b * nt + t, s * HGRN_HEADS + h))
    vec = pl.BlockSpec((1, HGRN_HEAD), lambda b, h, t: (0, h))
    return pl.pallas_call(
        functools.partial(_hgrn_kernel, tblk // HGRN_SUB),
        grid=(batch, HGRN_HEADS, nt),
        in_specs=[col(SLAB_QC), col(SLAB_FC), col(SLAB_IC), col(SLAB_ZC), vec, vec],
        out_specs=pl.BlockSpec((tblk, HGRN_HEAD), lambda b, h, t: (b * nt + t, h)),
        out_shape=jax.ShapeDtypeStruct((batch * seq, MIX_W), BF16),
        scratch_shapes=[pltpu.VMEM((HGRN_HEAD, HGRN_HEAD), F32),
                        pltpu.VMEM((HGRN_SUB, HGRN_HEAD), F32),
                        pltpu.VMEM((HGRN_SUB, HGRN_HEAD), F32),
                        pltpu.VMEM((HGRN_SUB, HGRN_HEAD), F32)],
        compiler_params=pltpu.CompilerParams(
            dimension_semantics=("parallel", "parallel", "arbitrary"),
            vmem_limit_bytes=VMEM_LIMIT),
        name="hgrn",
    )(hm, hm, hm, hm, lb, g_norm)


def _merge_kernel(final, *refs):
    sgu_refs, refs = refs[:7], refs[7:]
    if final:
        oa, oc, g0, g1, g2, x_ref, bm_ref, wb_ref, wo_ref, fg_ref, out_ref = refs
    else:
        oa, oc, g0, g1, g2, x_ref, bm_ref, wb_ref, wo_ref, out_ref = refs
    ob = _sgu_rows(x_ref.shape[0], *sgu_refs).astype(BF16)
    y = None
    for n, (o_val, gl_ref) in enumerate(((oa[...], g0), (ob, g1), (oc[...], g2))):
        proj = _dot(o_val, wb_ref[n])
        term = _sigmoid(gl_ref[...].astype(F32) + bm_ref[n:n + 1, :]) * proj
        y = term if y is None else y + term
    xo = x_ref[...] + _dot(y.astype(BF16), wo_ref[...])
    if final:
        ms = jnp.mean(xo * xo, axis=-1, keepdims=True)
        xo = xo * lax.rsqrt(ms + EPS) * fg_ref[...]
    out_ref[...] = xo


def _merge(oa, oc, hm, x2, sgu_params, bm, wb, wo, final_g):
    m = x2.shape[0]
    tm = min(MERGE_TM, m)
    final = final_g is not None
    act = pl.BlockSpec((tm, D_MODEL), lambda i: (i, 0))
    gl = lambda n: pl.BlockSpec((tm, D_MODEL), lambda i: (i, SLAB_GL + n))
    full = lambda a: pl.BlockSpec(a.shape, lambda i: (0,) * a.ndim)
    slab = lambda s_: pl.BlockSpec((tm, MIX_W), lambda i: (i, s_))
    in_specs = ([slab(SLAB_UB), slab(SLAB_VB), slab(SLAB_ZB)] + [full(p) for p in sgu_params]
                + [act, act, gl(0), gl(1), gl(2), act, full(bm), full(wb), full(wo)])
    args = [hm, hm, hm, *sgu_params, oa, oc, hm, hm, hm, x2, bm, wb, wo]
    if final:
        in_specs.append(full(final_g))
        args.append(final_g)
    return pl.pallas_call(
        functools.partial(_merge_kernel, final),
        grid=(m // tm,),
        in_specs=in_specs,
        out_specs=act,
        out_shape=jax.ShapeDtypeStruct((m, D_MODEL), F32),
        compiler_params=pltpu.CompilerParams(
            dimension_semantics=("parallel",), vmem_limit_bytes=VMEM_LIMIT),
        name="merge",
    )(*args)


def kernel(x, norm_g, w_in, b_merge, mu_shift, w0, w_w2, a0, a_w2, v0, v_w1, v_w2, k_k, k_a, r_k,
           lnx_g, lnx_b, sgu_ln_g, sgu_ln_b, sgu_w, sgu_b, lb_logits, hgrn_g, w_branch, w_out,
           final_g):
    batch, seq, _ = x.shape
    depth = w_in.shape[0]
    assert seq % min(HGRN_TBLK, seq) == 0 and seq % HGRN_SUB == 0 and x.shape[2] == D_MODEL
    assert seq % RWKV_TBLK == 0 and MERGE_TM % SGU_BLOCK == 0 and seq % SGU_BLOCK == 0
    n_shift = 3 * MIX_W + 2 * R_LORA

    lb_all = jnp.cumsum(jax.nn.softmax(lb_logits.astype(F32), axis=0), axis=0)
    lb_all = lb_all - lb_all[0]

    x2 = x.reshape(batch * seq, D_MODEL)
    vfirst = None
    for l in range(depth):
        w = w_in[l]
        wm = jnp.concatenate([w[:, :3 * MIX_W], w[:, n_shift:]], axis=1).astype(BF16)
        ext = [w[:, 3 * MIX_W:n_shift]]
        if l > 0:
            ext.append(v_w1[l - 1])
        we = jnp.concatenate(ext, axis=1)
        we = jnp.pad(we, ((0, 0), (0, N_EXT - we.shape[1]))).astype(BF16)

        zrow = jnp.zeros((MIX_W,), F32)
        mu = mu_shift[l]
        rows = [mu[0:MIX_W], mu[MIX_W:2 * MIX_W], mu[2 * MIX_W:3 * MIX_W], w0[l], a0[l], k_k[l],
                k_a[l], r_k[l], lnx_g[l], lnx_b[l],
                v0[l - 1] if l > 0 else zrow,
                jnp.pad(mu[3 * MIX_W:n_shift], (0, MIX_W - 2 * R_LORA))]
        assert len(rows) == PV_ROWS
        pvec = jnp.stack(rows + [zrow] * (2 * SUBLANES - PV_ROWS), axis=0)
        zl = jnp.zeros((R_LORA, MIX_W), F32)
        w2 = jnp.concatenate(
            [jnp.concatenate([w_w2[l], zl], axis=0), jnp.concatenate([zl, a_w2[l]], axis=0)],
            axis=1).astype(BF16)

        hm, he = _proj_in(x2, norm_g[l][None, :], wm, we)

        if l == 0:
            oa, vfirst = _rwkv(hm, he, None, pvec, w2, None, batch, seq)
        else:
            vw2 = jnp.pad(v_w2[l - 1], ((0, LANES - R_V), (0, 0))).astype(BF16)
            oa = _rwkv(hm, he, vfirst, pvec, w2, vw2, batch, seq)

        bias = jnp.repeat(sgu_b[l].T, LANES, axis=1)
        sgu_params = (sgu_ln_g[l][None, :], sgu_ln_b[l][None, :], sgu_w[l], bias)
        oc = _hgrn(hm, lb_all[l][None, :], hgrn_g[l][None, :], batch, seq)

        bm = jnp.pad(b_merge[l], ((0, SUBLANES - N_BRANCH), (0, 0)))
        x2 = _merge(oa, oc, hm, x2, sgu_params, bm, w_branch[l].astype(BF16), w_out[l].astype(BF16),
                    final_g[None, :] if l == depth - 1 else None)
    return x2.reshape(batch, seq, D_MODEL)
```

```python
import functools
import math

import jax
import jax.numpy as jnp
from jax import lax
from jax.experimental import pallas as pl
from jax.experimental.pallas import tpu as pltpu

F32 = jnp.float32
BF16 = jnp.bfloat16

D_MODEL = 1024
MIX_W = 1024
N_BRANCH = 3
CHUNK = 64
CHUNK_LOG2 = 6
RWKV_TBLK = 256
RWKV_HEAD_W = 64
R_LORA = 64
R_V = 32
SGU_BLOCK = 128
SGU_GROUPS = 8
HGRN_HEAD = 128
HGRN_HEADS = MIX_W // HGRN_HEAD
HGRN_STEP = 16
HGRN_STEP_LOG2 = 4
HGRN_SUB = 512
HGRN_TBLK = 4096
EPS = 1e-6
GN_EPS = 64e-5
LANES = 128
SUBLANES = 8
GROUP_W = 256
N_GROUPS = MIX_W // GROUP_W
N_MAIN = 14 * 1024
N_EXT = 256
VMEM_LIMIT = 48 * 1024 * 1024
PROJ_TM, PROJ_TN = 2048, 1024
MERGE_TM = 512
KK_NORM_FLOOR_SQ = 1e-24
(PV_MU_R, PV_MU_K, PV_MU_V, PV_W0, PV_A0, PV_KK, PV_KA, PV_RK, PV_LNG, PV_LNB, PV_V0,
 PV_MU_WA, PV_ROWS) = range(12 + 1)

SLAB_R, SLAB_K, SLAB_V, SLAB_ZA, SLAB_UB, SLAB_VB, SLAB_ZB = 0, 1, 2, 3, 4, 5, 6
SLAB_QC, SLAB_FC, SLAB_IC, SLAB_ZC, SLAB_GL = 7, 8, 9, 10, 11


def _dot(a, b):
    return jnp.dot(a, b, preferred_element_type=F32)


def _dot_nt(a, b):
    return lax.dot_general(a, b, (((1,), (1,)), ((), ())), preferred_element_type=F32)


def _dot_tn(a, b):
    return lax.dot_general(a, b, (((0,), (0,)), ((), ())), preferred_element_type=F32)


def _split_dot(sel, x, terms):
    acc = None
    rem = x
    for _ in range(terms):
        part = rem.astype(BF16)
        d = _dot(sel, part)
        acc = d if acc is None else acc + d
        rem = rem - part.astype(F32)
    return acc


def _sigmoid(x):
    return jax.nn.sigmoid(x)


def _silu(x):
    return x * jax.nn.sigmoid(x)


def _proj_kernel(x_ref, g_ref, wm_ref, we_ref, hm_ref, he_ref, xn_ref):
    @pl.when(pl.program_id(1) == 0)
    def _():
        x = x_ref[...]
        ms = jnp.mean(x * x, axis=-1, keepdims=True)
        xn = (x * lax.rsqrt(ms + EPS) * g_ref[...]).astype(BF16)
        xn_ref[...] = xn
        he_ref[...] = _dot(xn, we_ref[...])

    hm_ref[...] = _dot(xn_ref[...], wm_ref[...]).astype(hm_ref.dtype)


def _proj_in(x2, g, wm, we):
    m = x2.shape[0]
    tm, tn = min(PROJ_TM, m), PROJ_TN
    return pl.pallas_call(
        _proj_kernel,
        grid=(m // tm, N_MAIN // tn),
        in_specs=[
            pl.BlockSpec((tm, D_MODEL), lambda i, j: (i, 0)),
            pl.BlockSpec((1, D_MODEL), lambda i, j: (0, 0)),
            pl.BlockSpec((D_MODEL, tn), lambda i, j: (0, j)),
            pl.BlockSpec((D_MODEL, N_EXT), lambda i, j: (0, 0)),
        ],
        out_specs=[
            pl.BlockSpec((tm, tn), lambda i, j: (i, j)),
            pl.BlockSpec((tm, N_EXT), lambda i, j: (i, 0)),
        ],
        out_shape=[
            jax.ShapeDtypeStruct((m, N_MAIN), BF16),
            jax.ShapeDtypeStruct((m, N_EXT), F32),
        ],
        scratch_shapes=[pltpu.VMEM((tm, D_MODEL), BF16)],
        compiler_params=pltpu.CompilerParams(
            dimension_semantics=("parallel", "arbitrary"), vmem_limit_bytes=VMEM_LIMIT),
        name="proj_in",
    )(x2, g, wm, we)


def _rwkv_kernel(has_vres, *refs):
    if has_vres:
        (r_ref, k_ref, v_ref, z_ref, wa_ref, vl_ref, vf_ref, pv_ref, w2_ref, vw2_ref,
         o_ref, state_ref, cr_ref, ck_ref, cv_ref, cwa_ref) = refs
    else:
        (r_ref, k_ref, v_ref, z_ref, wa_ref, pv_ref, w2_ref,
         o_ref, vfo_ref, state_ref, cr_ref, ck_ref, cv_ref, cwa_ref) = refs

    @pl.when(pl.program_id(1) == 0)
    def _():
        for ref in (state_ref, cr_ref, ck_ref, cv_ref, cwa_ref):
            ref[...] = jnp.zeros_like(ref)

    row = lax.broadcasted_iota(jnp.int32, (RWKV_TBLK, 1), 0)

    def shift_mix(x_ref, c_ref, mu):
        x = x_ref[...].astype(F32)
        prev = c_ref[SUBLANES - 1:SUBLANES, :]
        xs = jnp.where(row == 0, prev, pltpu.roll(x, 1, 0))
        c_ref[...] = x[RWKV_TBLK - SUBLANES:RWKV_TBLK, :]
        return x + mu * (xs - x)

    pv = pv_ref[...]
    prow = lambda i: pv[i:i + 1]
    mu_r, mu_k, mu_v = prow(PV_MU_R), prow(PV_MU_K), prow(PV_MU_V)
    w0, a0, k_k, k_a, r_k = prow(PV_W0), prow(PV_A0), prow(PV_KK), prow(PV_KA), prow(PV_RK)
    lnx_g, lnx_b, v0 = prow(PV_LNG), prow(PV_LNB), prow(PV_V0)
    mu_wa = prow(PV_MU_WA)[:, 0:LANES]

    r = shift_mix(r_ref, cr_ref, mu_r)
    k = shift_mix(k_ref, ck_ref, mu_k)
    v = shift_mix(v_ref, cv_ref, mu_v)
    wa = shift_mix(wa_ref, cwa_ref, mu_wa)

    lane = lax.broadcasted_iota(jnp.int32, (1, LANES), 1)
    tl = jnp.where(lane < R_LORA, jnp.tanh(wa), wa).astype(BF16)
    pre = _dot(tl, w2_ref[...])
    lw = -math.exp(-0.5) * _sigmoid(w0 + pre[:, :MIX_W])
    a = _sigmoid(a0 + pre[:, MIX_W:])

    if has_vres:
        gate = _sigmoid(v0 + _dot(vl_ref[...].astype(BF16), vw2_ref[...]))
        v = v + (vf_ref[...] - v) * gate
    else:
        vfo_ref[...] = v

    ti = lax.broadcasted_iota(jnp.int32, (RWKV_TBLK, RWKV_TBLK), 0)
    si = lax.broadcasted_iota(jnp.int32, (RWKV_TBLK, RWKV_TBLK), 1)
    same_chunk = jnp.right_shift(ti, CHUNK_LOG2) == jnp.right_shift(si, CHUNK_LOG2)
    tri = jnp.where(same_chunk & (si <= ti), 1.0, 0.0).astype(BF16)
    cum = _split_dot(tri, lw, 2)
    ecum = jnp.exp(cum)
    ecum_prev = jnp.exp(cum - lw)
    einv = jnp.exp(-cum)
    chunks = range(RWKV_TBLK // CHUNK)
    rcs = [slice(c * CHUNK, (c + 1) * CHUNK) for c in chunks]
    cls = [cum[(c + 1) * CHUNK - 1:(c + 1) * CHUNK] for c in chunks]
    g_cs = [jnp.exp(cl) for cl in cls]

    groups = range(N_GROUPS)
    gls = [slice(g * GROUP_W, (g + 1) * GROUP_W) for g in groups]
    units = [(c, g) for c in chunks for g in groups]
    t_i = lax.broadcasted_iota(jnp.int32, (CHUNK, GROUP_W), 0)
    l_i = lax.broadcasted_iota(jnp.int32, (CHUNK, GROUP_W), 1)
    s_i = jnp.bitwise_and(l_i, RWKV_HEAD_W - 1)
    strict = t_i > s_i
    incl = t_i >= s_i
    eye = jnp.where(t_i == s_i, 1.0, 0.0).astype(F32)
    head_of_lane = jnp.right_shift(l_i, CHUNK_LOG2)
    head_masks = [jnp.where(head_of_lane == h, 1.0, 0.0).astype(BF16)
                  for h in range(GROUP_W // RWKV_HEAD_W)]
    bi = lax.broadcasted_iota(jnp.int32, (GROUP_W, GROUP_W), 0)
    bj = lax.broadcasted_iota(jnp.int32, (GROUP_W, GROUP_W), 1)
    same_head = jnp.right_shift(bi, CHUNK_LOG2) == jnp.right_shift(bj, CHUNK_LOG2)
    seg = jnp.where(same_head, 1.0, 0.0).astype(BF16)

    def stack4(x):
        xb = x.astype(BF16)
        return jnp.concatenate([xb * m for m in head_masks], axis=0)

    def segsum(x):
        return _dot(x.astype(BF16), seg)

    kk_raw = k * k_k
    k2 = k * (1.0 + (a - 1.0) * k_a)
    sums2 = [segsum(jnp.concatenate([kk_raw[:, gl] * kk_raw[:, gl],
                                     r[:, gl] * k2[:, gl] * r_k[:, gl]], axis=0)) for gl in gls]
    lhss, rhss, bks, v_sts = {}, {}, {}, {}
    for c, g in units:
        rc, gl = rcs[c], gls[g]
        kk = kk_raw[rc, gl] * lax.rsqrt(jnp.maximum(sums2[g][rc], KK_NORM_FLOOR_SQ))
        kka = kk * a[rc, gl]
        at = -kk * ecum_prev[rc, gl]
        bt = kka * einv[rc, gl]
        kt = k2[rc, gl] * einv[rc, gl]
        rt = r[rc, gl] * ecum[rc, gl]
        lhss[c, g] = jnp.concatenate([at, rt], axis=0).astype(BF16)
        rhss[c, g] = jnp.concatenate([stack4(bt), stack4(kt)], axis=0)
        bks[c, g] = jnp.concatenate([bt, kt], axis=0).astype(BF16)
        v_sts[c, g] = stack4(v[rc, gl])

    n_abs, g_rbs, gkvs = {}, {}, {}
    psts, pws, sks, tinvs, ahs, us, ys = {}, {}, {}, {}, {}, {}, {}

    def gram(sel):
        gs = {u: _dot_nt(lhss[u], rhss[u]) for u in sel}
        g_ks = {}
        for u in sel:
            n_abs[u] = jnp.where(strict, gs[u][0:CHUNK, 0:GROUP_W], 0.0)
            g_ks[u] = jnp.concatenate([jnp.where(strict, gs[u][0:CHUNK, GROUP_W:], 0.0),
                                       jnp.where(incl, gs[u][CHUNK:, GROUP_W:], 0.0)],
                                      axis=0).astype(BF16)
            g_rbs[u] = jnp.where(incl, gs[u][CHUNK:, 0:GROUP_W], 0.0).astype(BF16)
        for u in sel:
            gkvs[u] = _dot(g_ks[u], v_sts[u])

    def inv_start(sel):
        for u in sel:
            psts[u] = stack4(n_abs[u])
        for u in sel:
            pws[u] = _dot(n_abs[u].astype(BF16), psts[u])
            sks[u] = eye + n_abs[u]

    def inv_step(sel):
        for u in sel:
            psts[u] = stack4(pws[u])
        for u in sel:
            prod = _dot(jnp.concatenate([sks[u], pws[u]], axis=0).astype(BF16), psts[u])
            sks[u] = sks[u] + prod[0:CHUNK]
            pws[u] = prod[CHUNK:]

    def inv_end(sel):
        for u in sel:
            psts[u] = stack4(pws[u])
        for u in sel:
            tinvs[u] = (sks[u] + _dot(sks[u].astype(BF16), psts[u])).astype(BF16)

    states = [state_ref[g] for g in groups]

    def state_read(c):
        for g in groups:
            ahs[c, g] = _dot_nt(lhss[c, g], states[g].astype(BF16))

    def state_solve(c):
        for g in groups:
            us[c, g] = _dot(tinvs[c, g], stack4(ahs[c, g][0:CHUNK] + gkvs[c, g][0:CHUNK]))

    def state_update(c):
        for g in groups:
            ys[c, g] = (ahs[c, g][CHUNK:] + gkvs[c, g][CHUNK:]
                        + _dot(g_rbs[c, g], stack4(us[c, g])))
        for g in groups:
            uv = jnp.concatenate([us[c, g], v[rcs[c], gls[g]]], axis=0).astype(BF16)
            states[g] = ((states[g] + jnp.where(same_head, _dot_tn(uv, bks[c, g]), 0.0))
                         * g_cs[c][:, gls[g]])

    first = [u for u in units if u[0] < 2]
    second = [u for u in units if u[0] >= 2]
    gram(first)
    inv_start(first)
    for _ in range(4):
        inv_step(first)
    inv_end(first)
    gram(second)
    inv_start(second)
    for stage in (lambda: state_read(0), lambda: state_solve(0), lambda: state_update(0),
                  lambda: state_read(1)):
        stage()
        inv_step(second)
    state_solve(1)
    inv_end(second)
    state_update(1)
    for c in (2, 3):
        state_read(c)
        state_solve(c)
        state_update(c)
    state_ref[...] = jnp.stack(states, axis=0)

    y_rows = [jnp.concatenate([ys[c, g] for c in chunks], axis=0) for g in groups]
    means = [segsum(y) * (1.0 / RWKV_HEAD_W) for y in y_rows]
    ds = [y_rows[g] - means[g] for g in groups]
    variances = [segsum(d * d) * (1.0 / RWKV_HEAD_W) for d in ds]
    outs = []
    for g, gl in enumerate(gls):
        yn = ds[g] * lax.rsqrt(variances[g] + GN_EPS) * lnx_g[:, gl] + lnx_b[:, gl]
        bonus = sums2[g][RWKV_TBLK:] * v[:, gl]
        outs.append((yn + bonus) * _silu(z_ref[:, gl].astype(F32)))
    o_ref[...] = jnp.concatenate(outs, axis=1).astype(o_ref.dtype)


def _rwkv(hm, he, vfirst, pvec, w2, vw2, batch, seq):
    nt = seq // RWKV_TBLK
    has_vres = vfirst is not None
    tok = lambda b, t: b * nt + t
    slab = lambda s: pl.BlockSpec((RWKV_TBLK, MIX_W), lambda b, t: (tok(b, t), s))
    ext = lambda s: pl.BlockSpec((RWKV_TBLK, LANES), lambda b, t: (tok(b, t), s))
    full = lambda a: pl.BlockSpec(a.shape, lambda b, t: (0,) * a.ndim)
    act = pl.BlockSpec((RWKV_TBLK, MIX_W), lambda b, t: (tok(b, t), 0))
    in_specs = [slab(SLAB_R), slab(SLAB_K), slab(SLAB_V), slab(SLAB_ZA), ext(0)]
    args = [hm, hm, hm, hm, he]
    if has_vres:
        in_specs += [ext(1), act, full(pvec), full(w2), full(vw2)]
        args += [he, vfirst, pvec, w2, vw2]
        out_specs = act
        out_shape = jax.ShapeDtypeStruct((batch * seq, MIX_W), BF16)
    else:
        in_specs += [full(pvec), full(w2)]
        args += [pvec, w2]
        out_specs = [act, act]
        out_shape = [jax.ShapeDtypeStruct((batch * seq, MIX_W), BF16),
                     jax.ShapeDtypeStruct((batch * seq, MIX_W), F32)]
    return pl.pallas_call(
        functools.partial(_rwkv_kernel, has_vres),
        grid=(batch, nt),
        in_specs=in_specs,
        out_specs=out_specs,
        out_shape=out_shape,
        scratch_shapes=[
            pltpu.VMEM((N_GROUPS, GROUP_W, GROUP_W), F32),
            pltpu.VMEM((SUBLANES, MIX_W), F32),
            pltpu.VMEM((SUBLANES, MIX_W), F32),
            pltpu.VMEM((SUBLANES, MIX_W), F32),
            pltpu.VMEM((SUBLANES, LANES), F32),
        ],
        compiler_params=pltpu.CompilerParams(
            dimension_semantics=("parallel", "arbitrary"), vmem_limit_bytes=VMEM_LIMIT),
        name="rwkv",
    )(*args)


def _sgu_rows(n_rows, u_ref, v_ref, z_ref, lng_ref, lnb_ref, ws_ref, bias_ref):
    v = v_ref[...].astype(F32)
    mean = jnp.mean(v, axis=-1, keepdims=True)
    d = v - mean
    var = jnp.mean(d * d, axis=-1, keepdims=True)
    vn = (d * lax.rsqrt(var + EPS) * lng_ref[...] + lnb_ref[...]).astype(BF16)
    ti = lax.broadcasted_iota(jnp.int32, (SGU_BLOCK, SGU_BLOCK), 0)
    si = lax.broadcasted_iota(jnp.int32, (SGU_BLOCK, SGU_BLOCK), 1)
    causal = (ti >= CHUNK) | (si < CHUNK)
    cols = []
    for g in range(SGU_GROUPS):
        sl = slice(g * LANES, (g + 1) * LANES)
        ws = jnp.where(causal, ws_ref[g], 0.0).astype(BF16)
        blocks = []
        for b in range(n_rows // SGU_BLOCK):
            rows = slice(b * SGU_BLOCK, (b + 1) * SGU_BLOCK)
            mixed = _dot(ws, vn[rows, sl]) + bias_ref[:, sl]
            blocks.append(u_ref[rows, sl].astype(F32) * mixed * _silu(z_ref[rows, sl].astype(F32)))
        cols.append(jnp.concatenate(blocks, axis=0))
    return jnp.concatenate(cols, axis=1)


def _hgrn_kernel(n_sub, q_ref, f_ref, i_ref, z_ref, lb_ref, g_ref, o_ref, state_ref, kx_ref, fg_ref,
                 ic_ref):
    @pl.when(pl.program_id(2) == 0)
    def _():
        state_ref[...] = jnp.zeros_like(state_ref)

    for sub in range(n_sub):
        _hgrn_block(slice(sub * HGRN_SUB, (sub + 1) * HGRN_SUB), q_ref, f_ref, i_ref, z_ref,
                    lb_ref, g_ref, o_ref, state_ref, kx_ref, fg_ref, ic_ref)


def _hgrn_block(rows, q_ref, f_ref, i_ref, z_ref, lb_ref, g_ref, o_ref, state_ref, kx_ref, fg_ref,
                ic_ref):
    lb = lb_ref[...]
    gi = lax.broadcasted_iota(jnp.int32, (HGRN_SUB, HGRN_SUB), 0)
    gj = lax.broadcasted_iota(jnp.int32, (HGRN_SUB, HGRN_SUB), 1)
    same_step = jnp.right_shift(gi, HGRN_STEP_LOG2) == jnp.right_shift(gj, HGRN_STEP_LOG2)
    tri = jnp.where(same_step & (gj <= gi), 1.0, 0.0).astype(BF16)
    ones = jnp.ones((HGRN_HEAD, HGRN_HEAD), BF16)
    half = HGRN_STEP // 2
    trow = lax.broadcasted_iota(jnp.int32, (half, 1), 0)

    q = _silu(q_ref[rows, :].astype(F32))
    fg = lb + (1.0 - lb) * _sigmoid(f_ref[rows, :].astype(F32))
    kx = 1.0 - fg
    ic = i_ref[rows, :].astype(F32)
    kx_ref[...] = kx
    fg_ref[...] = fg
    ic_ref[...] = ic

    def row_bcast(ref, r):
        return jnp.broadcast_to(ref[r:r + 1, :], (half, HGRN_HEAD))

    bcs = _split_dot(tri, jnp.log(fg), 2)
    steps = range(HGRN_SUB // HGRN_STEP)
    rss = [slice(j * HGRN_STEP, (j + 1) * HGRN_STEP) for j in steps]
    bls = [bcs[rs][HGRN_STEP - 1:HGRN_STEP] for rs in rss]
    atts_lo, atts_hi = [], []
    for j, rs in enumerate(rss):
        qs, base = q[rs], j * HGRN_STEP
        q_lo, q_hi = qs[0:half], qs[half:HGRN_STEP]
        p_lo, p_hi = [None] * half, [None] * HGRN_STEP
        lo = hi = None
        for s in range(HGRN_STEP - 1, -1, -1):
            nxt = row_bcast(fg_ref, base + s + 1) if s + 1 < HGRN_STEP else None
            k_s = row_bcast(kx_ref, base + s)
            if s >= half:
                prev = 0.0 if hi is None else hi * nxt
                hi = jnp.where(trow == s - half, q_hi, prev)
            else:
                hi = hi * nxt
                prev = 0.0 if lo is None else lo * nxt
                lo = jnp.where(trow == s, q_lo, prev)
                p_lo[s] = lo * k_s
            p_hi[s] = hi * k_s
        stack = jnp.concatenate(p_lo + p_hi, axis=0).astype(BF16)
        att = _dot(stack, ones)
        atts_lo.append(att[0:half * half])
        atts_hi.append(att[half * half:])
    incs = [_dot_tn(ic[rs].astype(BF16), (kx[rs] * jnp.exp(bl - bcs[rs])).astype(BF16))
            for rs, bl in zip(rss, bls)]
    qb = (q * jnp.exp(bcs)).astype(BF16)
    st = state_ref[...]
    o_inters = []
    for j in steps:
        o_inters.append(_dot_nt(qb[rss[j]], st.astype(BF16)))
        st = st * jnp.exp(bls[j]) + incs[j]
    state_ref[...] = st
    outs = []
    for j, rs in enumerate(rss):
        a_lo, a_hi, base = atts_lo[j], atts_hi[j], j * HGRN_STEP
        o_lo = a_lo[0:half] * row_bcast(ic_ref, base)
        for s in range(1, half):
            o_lo = o_lo + a_lo[s * half:(s + 1) * half] * row_bcast(ic_ref, base + s)
        o_hi = a_hi[0:half] * row_bcast(ic_ref, base)
        for s in range(1, HGRN_STEP):
            o_hi = o_hi + a_hi[s * half:(s + 1) * half] * row_bcast(ic_ref, base + s)
        outs.append(o_inters[j] + jnp.concatenate([o_lo, o_hi], axis=0))
    o = jnp.concatenate(outs, axis=0)
    o = o * lax.rsqrt(jnp.mean(o * o, axis=-1, keepdims=True) + EPS) * g_ref[...]
    o_ref[rows, :] = (o * _silu(z_ref[rows, :].astype(F32))).astype(o_ref.dtype)


def _hgrn(hm, lb, g_norm, batch, seq):
    tblk = min(HGRN_TBLK, seq)
    nt = seq // tblk
    col = lambda s: pl.BlockSpec(
        (tblk, HGRN_HEAD), lambda b, h, t: (b * nt + t, s * HGRN_HEADS + h))
    vec = pl.BlockSpec((1, HGRN_HEAD), lambda b, h, t: (0, h))
    return pl.pallas_call(
        functools.partial(_hgrn_kernel, tblk // HGRN_SUB),
        grid=(batch, HGRN_HEADS, nt),
        in_specs=[col(SLAB_QC), col(SLAB_FC), col(SLAB_IC), col(SLAB_ZC), vec, vec],
        out_specs=pl.BlockSpec((tblk, HGRN_HEAD), lambda b, h, t: (b * nt + t, h)),
        out_shape=jax.ShapeDtypeStruct((batch * seq, MIX_W), BF16),
        scratch_shapes=[pltpu.VMEM((HGRN_HEAD, HGRN_HEAD), F32),
                        pltpu.VMEM((HGRN_SUB, HGRN_HEAD), F32),
                        pltpu.VMEM((HGRN_SUB, HGRN_HEAD), F32),
                        pltpu.VMEM((HGRN_SUB, HGRN_HEAD), F32)],
        compiler_params=pltpu.CompilerParams(
            dimension_semantics=("parallel", "parallel", "arbitrary"),
            vmem_limit_bytes=VMEM_LIMIT),
        name="hgrn",
    )(hm, hm, hm, hm, lb, g_norm)


def _merge_kernel(final, *refs):
    sgu_refs, refs = refs[:7], refs[7:]
    if final:
        oa, oc, g0, g1, g2, x_ref, bm_ref, wb_ref, wo_ref, fg_ref, out_ref = refs
    else:
        oa, oc, g0, g1, g2, x_ref, bm_ref, wb_ref, wo_ref, out_ref = refs
    ob = _sgu_rows(x_ref.shape[0], *sgu_refs).astype(BF16)
    y = None
    for n, (o_val, gl_ref) in enumerate(((oa[...], g0), (ob, g1), (oc[...], g2))):
        proj = _dot(o_val, wb_ref[n])
        term = _sigmoid(gl_ref[...].astype(F32) + bm_ref[n:n + 1, :]) * proj
        y = term if y is None else y + term
    xo = x_ref[...] + _dot(y.astype(BF16), wo_ref[...])
    if final:
        ms = jnp.mean(xo * xo, axis=-1, keepdims=True)
        xo = xo * lax.rsqrt(ms + EPS) * fg_ref[...]
    out_ref[...] = xo


def _merge(oa, oc, hm, x2, sgu_params, bm, wb, wo, final_g):
    m = x2.shape[0]
    tm = min(MERGE_TM, m)
    final = final_g is not None
    act = pl.BlockSpec((tm, D_MODEL), lambda i: (i, 0))
    gl = lambda n: pl.BlockSpec((tm, D_MODEL), lambda i: (i, SLAB_GL + n))
    full = lambda a: pl.BlockSpec(a.shape, lambda i: (0,) * a.ndim)
    slab = lambda s_: pl.BlockSpec((tm, MIX_W), lambda i: (i, s_))
    in_specs = ([slab(SLAB_UB), slab(SLAB_VB), slab(SLAB_ZB)] + [full(p) for p in sgu_params]
                + [act, act, gl(0), gl(1), gl(2), act, full(bm), full(wb), full(wo)])
    args = [hm, hm, hm, *sgu_params, oa, oc, hm, hm, hm, x2, bm, wb, wo]
    if final:
        in_specs.append(full(final_g))
        args.append(final_g)
    return pl.pallas_call(
        functools.partial(_merge_kernel, final),
        grid=(m // tm,),
        in_specs=in_specs,
        out_specs=act,
        out_shape=jax.ShapeDtypeStruct((m, D_MODEL), F32),
        compiler_params=pltpu.CompilerParams(
            dimension_semantics=("parallel",), vmem_limit_bytes=VMEM_LIMIT),
        name="merge",
    )(*args)


def kernel(x, norm_g, w_in, b_merge, mu_shift, w0, w_w2, a0, a_w2, v0, v_w1, v_w2, k_k, k_a, r_k,
           lnx_g, lnx_b, sgu_ln_g, sgu_ln_b, sgu_w, sgu_b, lb_logits, hgrn_g, w_branch, w_out,
           final_g):
    batch, seq, _ = x.shape
    depth = w_in.shape[0]
    assert seq % min(HGRN_TBLK, seq) == 0 and seq % HGRN_SUB == 0 and x.shape[2] == D_MODEL
    assert seq % RWKV_TBLK == 0 and MERGE_TM % SGU_BLOCK == 0 and seq % SGU_BLOCK == 0
    n_shift = 3 * MIX_W + 2 * R_LORA

    lb_all = jnp.cumsum(jax.nn.softmax(lb_logits.astype(F32), axis=0), axis=0)
    lb_all = lb_all - lb_all[0]

    x2 = x.reshape(batch * seq, D_MODEL)
    w_in_bf = w_in.astype(BF16)
    vfirst = None
    for l in range(depth):
        w = w_in_bf[l]
        wm = jnp.concatenate([w[:, :3 * MIX_W], w[:, n_shift:]], axis=1)
        ext = [w[:, 3 * MIX_W:n_shift]]
        if l > 0:
            ext.append(v_w1[l - 1].astype(BF16))
        we = jnp.concatenate(ext, axis=1)
        we = jnp.pad(we, ((0, 0), (0, N_EXT - we.shape[1])))

        zrow = jnp.zeros((MIX_W,), F32)
        mu = mu_shift[l]
        rows = [mu[0:MIX_W], mu[MIX_W:2 * MIX_W], mu[2 * MIX_W:3 * MIX_W], w0[l], a0[l], k_k[l],
                k_a[l], r_k[l], lnx_g[l], lnx_b[l],
                v0[l - 1] if l > 0 else zrow,
                jnp.pad(mu[3 * MIX_W:n_shift], (0, MIX_W - 2 * R_LORA))]
        assert len(rows) == PV_ROWS
        pvec = jnp.stack(rows + [zrow] * (2 * SUBLANES - PV_ROWS), axis=0)
        zl = jnp.zeros((R_LORA, MIX_W), F32)
        w2 = jnp.concatenate(
            [jnp.concatenate([w_w2[l], zl], axis=0), jnp.concatenate([zl, a_w2[l]], axis=0)],
            axis=1).astype(BF16)

        hm, he = _proj_in(x2, norm_g[l][None, :], wm, we)

        if l == 0:
            oa, vfirst = _rwkv(hm, he, None, pvec, w2, None, batch, seq)
        else:
            vw2 = jnp.pad(v_w2[l - 1], ((0, LANES - R_V), (0, 0))).astype(BF16)
            oa = _rwkv(hm, he, vfirst, pvec, w2, vw2, batch, seq)

        bias = jnp.repeat(sgu_b[l].T, LANES, axis=1)
        sgu_params = (sgu_ln_g[l][None, :], sgu_ln_b[l][None, :], sgu_w[l], bias)
        oc = _hgrn(hm, lb_all[l][None, :], hgrn_g[l][None, :], batch, seq)

        bm = jnp.pad(b_merge[l], ((0, SUBLANES - N_BRANCH), (0, 0)))
        x2 = _merge(oa, oc, hm, x2, sgu_params, bm, w_branch[l].astype(BF16), w_out[l].astype(BF16),
                    final_g[None, :] if l == depth - 1 else None)
    return x2.reshape(batch, seq, D_MODEL)
```

```python
import functools
import math

import jax
import jax.numpy as jnp
from jax import lax
from jax.experimental import pallas as pl
from jax.experimental.pallas import tpu as pltpu

F32 = jnp.float32
BF16 = jnp.bfloat16

D_MODEL = 1024
MIX_W = 1024
N_BRANCH = 3
CHUNK = 64
CHUNK_LOG2 = 6
RWKV_TBLK = 512
RWKV_HEAD_W = 64
R_LORA = 64
R_V = 32
SGU_BLOCK = 128
SGU_GROUPS = 8
HGRN_HEAD = 128
HGRN_HEADS = MIX_W // HGRN_HEAD
HGRN_STEP = 16
HGRN_STEP_LOG2 = 4
HGRN_SUB = 512
HGRN_TBLK = 4096
EPS = 1e-6
GN_EPS = 64e-5
LANES = 128
SUBLANES = 8
GROUP_W = 256
N_GROUPS = MIX_W // GROUP_W
N_MAIN = 14 * 1024
N_EXT = 256
VMEM_LIMIT = 48 * 1024 * 1024
PROJ_TM, PROJ_TN = 2048, 1024
MERGE_TM = 512
KK_NORM_FLOOR_SQ = 1e-24
(PV_MU_R, PV_MU_K, PV_MU_V, PV_W0, PV_A0, PV_KK, PV_KA, PV_RK, PV_LNG, PV_LNB, PV_V0,
 PV_MU_WA, PV_ROWS) = range(12 + 1)

SLAB_R, SLAB_K, SLAB_V, SLAB_ZA, SLAB_UB, SLAB_VB, SLAB_ZB = 0, 1, 2, 3, 4, 5, 6
SLAB_QC, SLAB_FC, SLAB_IC, SLAB_ZC, SLAB_GL = 7, 8, 9, 10, 11


def _dot(a, b):
    return jnp.dot(a, b, preferred_element_type=F32)


def _dot_nt(a, b):
    return lax.dot_general(a, b, (((1,), (1,)), ((), ())), preferred_element_type=F32)


def _dot_tn(a, b):
    return lax.dot_general(a, b, (((0,), (0,)), ((), ())), preferred_element_type=F32)


def _split_dot(sel, x, terms):
    acc = None
    rem = x
    for _ in range(terms):
        part = rem.astype(BF16)
        d = _dot(sel, part)
        acc = d if acc is None else acc + d
        rem = rem - part.astype(F32)
    return acc


def _sigmoid(x):
    return jax.nn.sigmoid(x)


def _silu(x):
    return x * jax.nn.sigmoid(x)


def _proj_kernel(x_ref, g_ref, wm_ref, we_ref, hm_ref, he_ref, xn_ref):
    @pl.when(pl.program_id(1) == 0)
    def _():
        x = x_ref[...]
        ms = jnp.mean(x * x, axis=-1, keepdims=True)
        xn = (x * lax.rsqrt(ms + EPS) * g_ref[...]).astype(BF16)
        xn_ref[...] = xn
        he_ref[...] = _dot(xn, we_ref[...])

    hm_ref[...] = _dot(xn_ref[...], wm_ref[...]).astype(hm_ref.dtype)


def _proj_in(x2, g, wm, we):
    m = x2.shape[0]
    tm, tn = min(PROJ_TM, m), PROJ_TN
    return pl.pallas_call(
        _proj_kernel,
        grid=(m // tm, N_MAIN // tn),
        in_specs=[
            pl.BlockSpec((tm, D_MODEL), lambda i, j: (i, 0)),
            pl.BlockSpec((1, D_MODEL), lambda i, j: (0, 0)),
            pl.BlockSpec((D_MODEL, tn), lambda i, j: (0, j)),
            pl.BlockSpec((D_MODEL, N_EXT), lambda i, j: (0, 0)),
        ],
        out_specs=[
            pl.BlockSpec((tm, tn), lambda i, j: (i, j)),
            pl.BlockSpec((tm, N_EXT), lambda i, j: (i, 0)),
        ],
        out_shape=[
            jax.ShapeDtypeStruct((m, N_MAIN), BF16),
            jax.ShapeDtypeStruct((m, N_EXT), F32),
        ],
        scratch_shapes=[pltpu.VMEM((tm, D_MODEL), BF16)],
        compiler_params=pltpu.CompilerParams(
            dimension_semantics=("parallel", "arbitrary"), vmem_limit_bytes=VMEM_LIMIT),
        name="proj_in",
    )(x2, g, wm, we)


def _rwkv_kernel(has_vres, *refs):
    if has_vres:
        (r_ref, k_ref, v_ref, z_ref, wa_ref, vl_ref, vf_ref, pv_ref, w2_ref, vw2_ref,
         o_ref, state_ref, cr_ref, ck_ref, cv_ref, cwa_ref) = refs
    else:
        (r_ref, k_ref, v_ref, z_ref, wa_ref, pv_ref, w2_ref,
         o_ref, vfo_ref, state_ref, cr_ref, ck_ref, cv_ref, cwa_ref) = refs

    @pl.when(pl.program_id(1) == 0)
    def _():
        for ref in (state_ref, cr_ref, ck_ref, cv_ref, cwa_ref):
            ref[...] = jnp.zeros_like(ref)

    row = lax.broadcasted_iota(jnp.int32, (RWKV_TBLK, 1), 0)

    def shift_mix(x_ref, c_ref, mu):
        x = x_ref[...].astype(F32)
        prev = c_ref[SUBLANES - 1:SUBLANES, :]
        xs = jnp.where(row == 0, prev, pltpu.roll(x, 1, 0))
        c_ref[...] = x[RWKV_TBLK - SUBLANES:RWKV_TBLK, :]
        return x + mu * (xs - x)

    pv = pv_ref[...]
    prow = lambda i: pv[i:i + 1]
    mu_r, mu_k, mu_v = prow(PV_MU_R), prow(PV_MU_K), prow(PV_MU_V)
    w0, a0, k_k, k_a, r_k = prow(PV_W0), prow(PV_A0), prow(PV_KK), prow(PV_KA), prow(PV_RK)
    lnx_g, lnx_b, v0 = prow(PV_LNG), prow(PV_LNB), prow(PV_V0)
    mu_wa = prow(PV_MU_WA)[:, 0:LANES]

    r = shift_mix(r_ref, cr_ref, mu_r)
    k = shift_mix(k_ref, ck_ref, mu_k)
    v = shift_mix(v_ref, cv_ref, mu_v)
    wa = shift_mix(wa_ref, cwa_ref, mu_wa)

    lane = lax.broadcasted_iota(jnp.int32, (1, LANES), 1)
    tl = jnp.where(lane < R_LORA, jnp.tanh(wa), wa).astype(BF16)
    pre = _dot(tl, w2_ref[...])
    lw = -math.exp(-0.5) * _sigmoid(w0 + pre[:, :MIX_W])
    a = _sigmoid(a0 + pre[:, MIX_W:])

    if has_vres:
        gate = _sigmoid(v0 + _dot(vl_ref[...].astype(BF16), vw2_ref[...]))
        v = v + (vf_ref[...] - v) * gate
    else:
        vfo_ref[...] = v

    ti = lax.broadcasted_iota(jnp.int32, (RWKV_TBLK, RWKV_TBLK), 0)
    si = lax.broadcasted_iota(jnp.int32, (RWKV_TBLK, RWKV_TBLK), 1)
    same_chunk = jnp.right_shift(ti, CHUNK_LOG2) == jnp.right_shift(si, CHUNK_LOG2)
    tri = jnp.where(same_chunk & (si <= ti), 1.0, 0.0).astype(BF16)
    cum = _split_dot(tri, lw, 2)
    ecum = jnp.exp(cum)
    ecum_prev = jnp.exp(cum - lw)
    einv = jnp.exp(-cum)
    chunks = range(RWKV_TBLK // CHUNK)
    rcs = [slice(c * CHUNK, (c + 1) * CHUNK) for c in chunks]
    cls = [cum[(c + 1) * CHUNK - 1:(c + 1) * CHUNK] for c in chunks]
    g_cs = [jnp.exp(cl) for cl in cls]

    groups = range(N_GROUPS)
    gls = [slice(g * GROUP_W, (g + 1) * GROUP_W) for g in groups]
    units = [(c, g) for c in chunks for g in groups]
    t_i = lax.broadcasted_iota(jnp.int32, (CHUNK, GROUP_W), 0)
    l_i = lax.broadcasted_iota(jnp.int32, (CHUNK, GROUP_W), 1)
    s_i = jnp.bitwise_and(l_i, RWKV_HEAD_W - 1)
    strict = t_i > s_i
    incl = t_i >= s_i
    eye = jnp.where(t_i == s_i, 1.0, 0.0).astype(F32)
    head_of_lane = jnp.right_shift(l_i, CHUNK_LOG2)
    head_masks = [jnp.where(head_of_lane == h, 1.0, 0.0).astype(BF16)
                  for h in range(GROUP_W // RWKV_HEAD_W)]
    bi = lax.broadcasted_iota(jnp.int32, (GROUP_W, GROUP_W), 0)
    bj = lax.broadcasted_iota(jnp.int32, (GROUP_W, GROUP_W), 1)
    same_head = jnp.right_shift(bi, CHUNK_LOG2) == jnp.right_shift(bj, CHUNK_LOG2)
    seg = jnp.where(same_head, 1.0, 0.0).astype(BF16)

    def stack4(x):
        xb = x.astype(BF16)
        return jnp.concatenate([xb * m for m in head_masks], axis=0)

    def segsum(x):
        return _dot(x.astype(BF16), seg)

    kk_raw = k * k_k
    k2 = k * (1.0 + (a - 1.0) * k_a)
    sums2 = [segsum(jnp.concatenate([kk_raw[:, gl] * kk_raw[:, gl],
                                     r[:, gl] * k2[:, gl] * r_k[:, gl]], axis=0)) for gl in gls]
    lhss, rhss, bks, v_sts = {}, {}, {}, {}
    for c, g in units:
        rc, gl = rcs[c], gls[g]
        kk = kk_raw[rc, gl] * lax.rsqrt(jnp.maximum(sums2[g][rc], KK_NORM_FLOOR_SQ))
        kka = kk * a[rc, gl]
        at = -kk * ecum_prev[rc, gl]
        bt = kka * einv[rc, gl]
        kt = k2[rc, gl] * einv[rc, gl]
        rt = r[rc, gl] * ecum[rc, gl]
        lhss[c, g] = jnp.concatenate([at, rt], axis=0).astype(BF16)
        rhss[c, g] = jnp.concatenate([stack4(bt), stack4(kt)], axis=0)
        bks[c, g] = jnp.concatenate([bt, kt], axis=0).astype(BF16)
        v_sts[c, g] = stack4(v[rc, gl])

    gs = {u: _dot_nt(lhss[u], rhss[u]) for u in units}
    n_abs = {u: jnp.where(strict, gs[u][0:CHUNK, 0:GROUP_W], 0.0) for u in units}
    g_ks = {u: jnp.concatenate([jnp.where(strict, gs[u][0:CHUNK, GROUP_W:], 0.0),
                                jnp.where(incl, gs[u][CHUNK:, GROUP_W:], 0.0)],
                               axis=0).astype(BF16) for u in units}
    g_rbs = {u: jnp.where(incl, gs[u][CHUNK:, 0:GROUP_W], 0.0).astype(BF16) for u in units}
    gkvs = {u: _dot(g_ks[u], v_sts[u]) for u in units}

    psts = {u: stack4(n_abs[u]) for u in units}
    pws = {u: _dot(n_abs[u].astype(BF16), psts[u]) for u in units}
    sks = {u: eye + n_abs[u] for u in units}
    for _ in range(4):
        psts = {u: stack4(pws[u]) for u in units}
        prods = {u: _dot(jnp.concatenate([sks[u], pws[u]], axis=0).astype(BF16), psts[u])
                 for u in units}
        sks = {u: sks[u] + prods[u][0:CHUNK] for u in units}
        pws = {u: prods[u][CHUNK:] for u in units}
    psts = {u: stack4(pws[u]) for u in units}
    tinvs = {u: (sks[u] + _dot(sks[u].astype(BF16), psts[u])).astype(BF16) for u in units}

    states = [state_ref[g] for g in groups]
    ys = {}
    for c in chunks:
        ahs = [_dot_nt(lhss[c, g], states[g].astype(BF16)) for g in groups]
        us = [_dot(tinvs[c, g], stack4(ahs[g][0:CHUNK] + gkvs[c, g][0:CHUNK])) for g in groups]
        for g in groups:
            ys[c, g] = (ahs[g][CHUNK:] + gkvs[c, g][CHUNK:]
                        + _dot(g_rbs[c, g], stack4(us[g])))
        uvs = [jnp.concatenate([us[g], v[rcs[c], gls[g]]], axis=0).astype(BF16) for g in groups]
        states = [(states[g] + jnp.where(same_head, _dot_tn(uvs[g], bks[c, g]), 0.0))
                  * g_cs[c][:, gls[g]] for g in groups]
    state_ref[...] = jnp.stack(states, axis=0)

    y_rows = [jnp.concatenate([ys[c, g] for c in chunks], axis=0) for g in groups]
    means = [segsum(y) * (1.0 / RWKV_HEAD_W) for y in y_rows]
    ds = [y_rows[g] - means[g] for g in groups]
    variances = [segsum(d * d) * (1.0 / RWKV_HEAD_W) for d in ds]
    outs = []
    for g, gl in enumerate(gls):
        yn = ds[g] * lax.rsqrt(variances[g] + GN_EPS) * lnx_g[:, gl] + lnx_b[:, gl]
        bonus = sums2[g][RWKV_TBLK:] * v[:, gl]
        outs.append((yn + bonus) * _silu(z_ref[:, gl].astype(F32)))
    o_ref[...] = jnp.concatenate(outs, axis=1).astype(o_ref.dtype)


def _rwkv(hm, he, vfirst, pvec, w2, vw2, batch, seq):
    nt = seq // RWKV_TBLK
    has_vres = vfirst is not None
    tok = lambda b, t: b * nt + t
    slab = lambda s: pl.BlockSpec((RWKV_TBLK, MIX_W), lambda b, t: (tok(b, t), s))
    ext = lambda s: pl.BlockSpec((RWKV_TBLK, LANES), lambda b, t: (tok(b, t), s))
    full = lambda a: pl.BlockSpec(a.shape, lambda b, t: (0,) * a.ndim)
    act = pl.BlockSpec((RWKV_TBLK, MIX_W), lambda b, t: (tok(b, t), 0))
    in_specs = [slab(SLAB_R), slab(SLAB_K), slab(SLAB_V), slab(SLAB_ZA), ext(0)]
    args = [hm, hm, hm, hm, he]
    if has_vres:
        in_specs += [ext(1), act, full(pvec), full(w2), full(vw2)]
        args += [he, vfirst, pvec, w2, vw2]
        out_specs = act
        out_shape = jax.ShapeDtypeStruct((batch * seq, MIX_W), BF16)
    else:
        in_specs += [full(pvec), full(w2)]
        args += [pvec, w2]
        out_specs = [act, act]
        out_shape = [jax.ShapeDtypeStruct((batch * seq, MIX_W), BF16),
                     jax.ShapeDtypeStruct((batch * seq, MIX_W), F32)]
    return pl.pallas_call(
        functools.partial(_rwkv_kernel, has_vres),
        grid=(batch, nt),
        in_specs=in_specs,
        out_specs=out_specs,
        out_shape=out_shape,
        scratch_shapes=[
            pltpu.VMEM((N_GROUPS, GROUP_W, GROUP_W), F32),
            pltpu.VMEM((SUBLANES, MIX_W), F32),
            pltpu.VMEM((SUBLANES, MIX_W), F32),
            pltpu.VMEM((SUBLANES, MIX_W), F32),
            pltpu.VMEM((SUBLANES, LANES), F32),
        ],
        compiler_params=pltpu.CompilerParams(
            dimension_semantics=("parallel", "arbitrary"), vmem_limit_bytes=VMEM_LIMIT),
        name="rwkv",
    )(*args)


def _sgu_rows(n_rows, u_ref, v_ref, z_ref, lng_ref, lnb_ref, ws_ref, bias_ref):
    v = v_ref[...].astype(F32)
    mean = jnp.mean(v, axis=-1, keepdims=True)
    d = v - mean
    var = jnp.mean(d * d, axis=-1, keepdims=True)
    vn = (d * lax.rsqrt(var + EPS) * lng_ref[...] + lnb_ref[...]).astype(BF16)
    ti = lax.broadcasted_iota(jnp.int32, (SGU_BLOCK, SGU_BLOCK), 0)
    si = lax.broadcasted_iota(jnp.int32, (SGU_BLOCK, SGU_BLOCK), 1)
    causal = (ti >= CHUNK) | (si < CHUNK)
    cols = []
    for g in range(SGU_GROUPS):
        sl = slice(g * LANES, (g + 1) * LANES)
        ws = jnp.where(causal, ws_ref[g], 0.0).astype(BF16)
        blocks = []
        for b in range(n_rows // SGU_BLOCK):
            rows = slice(b * SGU_BLOCK, (b + 1) * SGU_BLOCK)
            mixed = _dot(ws, vn[rows, sl]) + bias_ref[:, sl]
            blocks.append(u_ref[rows, sl].astype(F32) * mixed * _silu(z_ref[rows, sl].astype(F32)))
        cols.append(jnp.concatenate(blocks, axis=0))
    return jnp.concatenate(cols, axis=1)


def _hgrn_kernel(n_sub, q_ref, f_ref, i_ref, z_ref, lb_ref, g_ref, o_ref, state_ref, kx_ref, fg_ref,
                 ic_ref):
    @pl.when(pl.program_id(2) == 0)
    def _():
        state_ref[...] = jnp.zeros_like(state_ref)

    for sub in range(n_sub):
        _hgrn_block(slice(sub * HGRN_SUB, (sub + 1) * HGRN_SUB), q_ref, f_ref, i_ref, z_ref,
                    lb_ref, g_ref, o_ref, state_ref, kx_ref, fg_ref, ic_ref)


def _hgrn_block(rows, q_ref, f_ref, i_ref, z_ref, lb_ref, g_ref, o_ref, state_ref, kx_ref, fg_ref,
                ic_ref):
    lb = lb_ref[...]
    gi = lax.broadcasted_iota(jnp.int32, (HGRN_SUB, HGRN_SUB), 0)
    gj = lax.broadcasted_iota(jnp.int32, (HGRN_SUB, HGRN_SUB), 1)
    same_step = jnp.right_shift(gi, HGRN_STEP_LOG2) == jnp.right_shift(gj, HGRN_STEP_LOG2)
    tri = jnp.where(same_step & (gj <= gi), 1.0, 0.0).astype(BF16)
    ones = jnp.ones((HGRN_HEAD, HGRN_HEAD), BF16)
    half = HGRN_STEP // 2
    trow = lax.broadcasted_iota(jnp.int32, (half, 1), 0)

    q = _silu(q_ref[rows, :].astype(F32))
    fg = lb + (1.0 - lb) * _sigmoid(f_ref[rows, :].astype(F32))
    kx = 1.0 - fg
    ic = i_ref[rows, :].astype(F32)
    kx_ref[...] = kx
    fg_ref[...] = fg
    ic_ref[...] = ic

    def row_bcast(ref, r):
        return jnp.broadcast_to(ref[r:r + 1, :], (half, HGRN_HEAD))

    bcs = _split_dot(tri, jnp.log(fg), 2)
    steps = range(HGRN_SUB // HGRN_STEP)
    rss = [slice(j * HGRN_STEP, (j + 1) * HGRN_STEP) for j in steps]
    bls = [bcs[rs][HGRN_STEP - 1:HGRN_STEP] for rs in rss]
    atts_lo, atts_hi = [], []
    for j, rs in enumerate(rss):
        qs, base = q[rs], j * HGRN_STEP
        q_lo, q_hi = qs[0:half], qs[half:HGRN_STEP]
        p_lo, p_hi = [None] * half, [None] * HGRN_STEP
        lo = hi = None
        for s in range(HGRN_STEP - 1, -1, -1):
            nxt = row_bcast(fg_ref, base + s + 1) if s + 1 < HGRN_STEP else None
            k_s = row_bcast(kx_ref, base + s)
            if s >= half:
                prev = 0.0 if hi is None else hi * nxt
                hi = jnp.where(trow == s - half, q_hi, prev)
            else:
                hi = hi * nxt
                prev = 0.0 if lo is None else lo * nxt
                lo = jnp.where(trow == s, q_lo, prev)
                p_lo[s] = lo * k_s
            p_hi[s] = hi * k_s
        stack = jnp.concatenate(p_lo + p_hi, axis=0).astype(BF16)
        att = _dot(stack, ones)
        atts_lo.append(att[0:half * half])
        atts_hi.append(att[half * half:])
    incs = [_dot_tn(ic[rs].astype(BF16), (kx[rs] * jnp.exp(bl - bcs[rs])).astype(BF16))
            for rs, bl in zip(rss, bls)]
    qb = (q * jnp.exp(bcs)).astype(BF16)
    st = state_ref[...]
    o_inters = []
    for j in steps:
        o_inters.append(_dot_nt(qb[rss[j]], st.astype(BF16)))
        st = st * jnp.exp(bls[j]) + incs[j]
    state_ref[...] = st
    outs = []
    for j, rs in enumerate(rss):
        a_lo, a_hi, base = atts_lo[j], atts_hi[j], j * HGRN_STEP
        o_lo = a_lo[0:half] * row_bcast(ic_ref, base)
        for s in range(1, half):
            o_lo = o_lo + a_lo[s * half:(s + 1) * half] * row_bcast(ic_ref, base + s)
        o_hi = a_hi[0:half] * row_bcast(ic_ref, base)
        for s in range(1, HGRN_STEP):
            o_hi = o_hi + a_hi[s * half:(s + 1) * half] * row_bcast(ic_ref, base + s)
        outs.append(o_inters[j] + jnp.concatenate([o_lo, o_hi], axis=0))
    o = jnp.concatenate(outs, axis=0)
    o = o * lax.rsqrt(jnp.mean(o * o, axis=-1, keepdims=True) + EPS) * g_ref[...]
    o_ref[rows, :] = (o * _silu(z_ref[rows, :].astype(F32))).astype(o_ref.dtype)


def _hgrn(hm, lb, g_norm, batch, seq):
    tblk = min(HGRN_TBLK, seq)
    nt = seq // tblk
    col = lambda s: pl.BlockSpec(
        (tblk, HGRN_HEAD), lambda b, h, t: (b * nt + t, s * HGRN_HEADS + h))
    vec = pl.BlockSpec((1, HGRN_HEAD), lambda b, h, t: (0, h))
    return pl.pallas_call(
        functools.partial(_hgrn_kernel, tblk // HGRN_SUB),
        grid=(batch, HGRN_HEADS, nt),
        in_specs=[col(SLAB_QC), col(SLAB_FC), col(SLAB_IC), col(SLAB_ZC), vec, vec],
        out_specs=pl.BlockSpec((tblk, HGRN_HEAD), lambda b, h, t: (b * nt + t, h)),
        out_shape=jax.ShapeDtypeStruct((batch * seq, MIX_W), BF16),
        scratch_shapes=[pltpu.VMEM((HGRN_HEAD, HGRN_HEAD), F32),
                        pltpu.VMEM((HGRN_SUB, HGRN_HEAD), F32),
                        pltpu.VMEM((HGRN_SUB, HGRN_HEAD), F32),
                        pltpu.VMEM((HGRN_SUB, HGRN_HEAD), F32)],
        compiler_params=pltpu.CompilerParams(
            dimension_semantics=("parallel", "parallel", "arbitrary"),
            vmem_limit_bytes=VMEM_LIMIT),
        name="hgrn",
    )(hm, hm, hm, hm, lb, g_norm)


def _merge_kernel(final, *refs):
    sgu_refs, refs = refs[:7], refs[7:]
    if final:
        oa, oc, g0, g1, g2, x_ref, bm_ref, wb_ref, wo_ref, fg_ref, out_ref = refs
    else:
        oa, oc, g0, g1, g2, x_ref, bm_ref, wb_ref, wo_ref, out_ref = refs
    ob = _sgu_rows(x_ref.shape[0], *sgu_refs).astype(BF16)
    y = None
    for n, (o_val, gl_ref) in enumerate(((oa[...], g0), (ob, g1), (oc[...], g2))):
        proj = _dot(o_val, wb_ref[n])
        term = _sigmoid(gl_ref[...].astype(F32) + bm_ref[n:n + 1, :]) * proj
        y = term if y is None else y + term
    xo = x_ref[...] + _dot(y.astype(BF16), wo_ref[...])
    if final:
        ms = jnp.mean(xo * xo, axis=-1, keepdims=True)
        xo = xo * lax.rsqrt(ms + EPS) * fg_ref[...]
    out_ref[...] = xo


def _merge(oa, oc, hm, x2, sgu_params, bm, wb, wo, final_g):
    m = x2.shape[0]
    tm = min(MERGE_TM, m)
    final = final_g is not None
    act = pl.BlockSpec((tm, D_MODEL), lambda i: (i, 0))
    gl = lambda n: pl.BlockSpec((tm, D_MODEL), lambda i: (i, SLAB_GL + n))
    full = lambda a: pl.BlockSpec(a.shape, lambda i: (0,) * a.ndim)
    slab = lambda s_: pl.BlockSpec((tm, MIX_W), lambda i: (i, s_))
    in_specs = ([slab(SLAB_UB), slab(SLAB_VB), slab(SLAB_ZB)] + [full(p) for p in sgu_params]
                + [act, act, gl(0), gl(1), gl(2), act, full(bm), full(wb), full(wo)])
    args = [hm, hm, hm, *sgu_params, oa, oc, hm, hm, hm, x2, bm, wb, wo]
    if final:
        in_specs.append(full(final_g))
        args.append(final_g)
    return pl.pallas_call(
        functools.partial(_merge_kernel, final),
        grid=(m // tm,),
        in_specs=in_specs,
        out_specs=act,
        out_shape=jax.ShapeDtypeStruct((m, D_MODEL), F32),
        compiler_params=pltpu.CompilerParams(
            dimension_semantics=("parallel",), vmem_limit_bytes=VMEM_LIMIT),
        name="merge",
    )(*args)


def kernel(x, norm_g, w_in, b_merge, mu_shift, w0, w_w2, a0, a_w2, v0, v_w1, v_w2, k_k, k_a, r_k,
           lnx_g, lnx_b, sgu_ln_g, sgu_ln_b, sgu_w, sgu_b, lb_logits, hgrn_g, w_branch, w_out,
           final_g):
    batch, seq, _ = x.shape
    depth = w_in.shape[0]
    assert seq % min(HGRN_TBLK, seq) == 0 and seq % HGRN_SUB == 0 and x.shape[2] == D_MODEL
    assert seq % RWKV_TBLK == 0 and MERGE_TM % SGU_BLOCK == 0 and seq % SGU_BLOCK == 0
    n_shift = 3 * MIX_W + 2 * R_LORA

    lb_all = jnp.cumsum(jax.nn.softmax(lb_logits.astype(F32), axis=0), axis=0)
    lb_all = lb_all - lb_all[0]

    x2 = x.reshape(batch * seq, D_MODEL)
    vfirst = None
    for l in range(depth):
        w = w_in[l]
        wm = jnp.concatenate([w[:, :3 * MIX_W], w[:, n_shift:]], axis=1).astype(BF16)
        ext = [w[:, 3 * MIX_W:n_shift]]
        if l > 0:
            ext.append(v_w1[l - 1])
        we = jnp.concatenate(ext, axis=1)
        we = jnp.pad(we, ((0, 0), (0, N_EXT - we.shape[1]))).astype(BF16)

        zrow = jnp.zeros((MIX_W,), F32)
        mu = mu_shift[l]
        rows = [mu[0:MIX_W], mu[MIX_W:2 * MIX_W], mu[2 * MIX_W:3 * MIX_W], w0[l], a0[l], k_k[l],
                k_a[l], r_k[l], lnx_g[l], lnx_b[l],
                v0[l - 1] if l > 0 else zrow,
                jnp.pad(mu[3 * MIX_W:n_shift], (0, MIX_W - 2 * R_LORA))]
        assert len(rows) == PV_ROWS
        pvec = jnp.stack(rows + [zrow] * (2 * SUBLANES - PV_ROWS), axis=0)
        zl = jnp.zeros((R_LORA, MIX_W), F32)
        w2 = jnp.concatenate(
            [jnp.concatenate([w_w2[l], zl], axis=0), jnp.concatenate([zl, a_w2[l]], axis=0)],
            axis=1).astype(BF16)

        hm, he = _proj_in(x2, norm_g[l][None, :], wm, we)

        if l == 0:
            oa, vfirst = _rwkv(hm, he, None, pvec, w2, None, batch, seq)
        else:
            vw2 = jnp.pad(v_w2[l - 1], ((0, LANES - R_V), (0, 0))).astype(BF16)
            oa = _rwkv(hm, he, vfirst, pvec, w2, vw2, batch, seq)

        bias = jnp.repeat(sgu_b[l].T, LANES, axis=1)
        sgu_params = (sgu_ln_g[l][None, :], sgu_ln_b[l][None, :], sgu_w[l], bias)
        oc = _hgrn(hm, lb_all[l][None, :], hgrn_g[l][None, :], batch, seq)

        bm = jnp.pad(b_merge[l], ((0, SUBLANES - N_BRANCH), (0, 0)))
        x2 = _merge(oa, oc, hm, x2, sgu_params, bm, w_branch[l].astype(BF16), w_out[l].astype(BF16),
                    final_g[None, :] if l == depth - 1 else None)
    return x2.reshape(batch, seq, D_MODEL)
```

```python
import functools
import math

import jax
import jax.numpy as jnp
from jax import lax
from jax.experimental import pallas as pl
from jax.experimental.pallas import tpu as pltpu

F32 = jnp.float32
BF16 = jnp.bfloat16

D_MODEL = 1024
MIX_W = 1024
N_BRANCH = 3
CHUNK = 64
CHUNK_LOG2 = 6
RWKV_TBLK = 512
RWKV_HEAD_W = 64
R_LORA = 64
R_V = 32
SGU_BLOCK = 128
SGU_GROUPS = 8
HGRN_HEAD = 128
HGRN_HEADS = MIX_W // HGRN_HEAD
HGRN_STEP = 16
HGRN_STEP_LOG2 = 4
HGRN_SUB = 512
HGRN_TBLK = 4096
EPS = 1e-6
GN_EPS = 64e-5
LANES = 128
SUBLANES = 8
GROUP_W = 256
N_GROUPS = MIX_W // GROUP_W
N_MAIN = 14 * 1024
N_EXT = 256
VMEM_LIMIT = 48 * 1024 * 1024
PROJ_TM, PROJ_TN = 2048, 1024
MERGE_TM = 512
KK_NORM_FLOOR_SQ = 1e-24
(PV_MU_R, PV_MU_K, PV_MU_V, PV_W0, PV_A0, PV_KK, PV_KA, PV_RK, PV_LNG, PV_LNB, PV_V0,
 PV_MU_WA, PV_ROWS) = range(12 + 1)

SLAB_R, SLAB_K, SLAB_V, SLAB_ZA, SLAB_UB, SLAB_VB, SLAB_ZB = 0, 1, 2, 3, 4, 5, 6
SLAB_QC, SLAB_FC, SLAB_IC, SLAB_ZC, SLAB_GL = 7, 8, 9, 10, 11


def _dot(a, b):
    return jnp.dot(a, b, preferred_element_type=F32)


def _dot_nt(a, b):
    return lax.dot_general(a, b, (((1,), (1,)), ((), ())), preferred_element_type=F32)


def _dot_tn(a, b):
    return lax.dot_general(a, b, (((0,), (0,)), ((), ())), preferred_element_type=F32)


def _split_dot(sel, x, terms):
    acc = None
    rem = x
    for _ in range(terms):
        part = rem.astype(BF16)
        d = _dot(sel, part)
        acc = d if acc is None else acc + d
        rem = rem - part.astype(F32)
    return acc


def _sigmoid(x):
    return jax.nn.sigmoid(x)


def _silu(x):
    return x * jax.nn.sigmoid(x)


def _proj_kernel(x_ref, g_ref, wm_ref, we_ref, hm_ref, he_ref, xn_ref):
    @pl.when(pl.program_id(1) == 0)
    def _():
        x = x_ref[...]
        ms = jnp.mean(x * x, axis=-1, keepdims=True)
        xn = (x * lax.rsqrt(ms + EPS) * g_ref[...]).astype(BF16)
        xn_ref[...] = xn
        he_ref[...] = _dot(xn, we_ref[...])

    hm_ref[...] = _dot(xn_ref[...], wm_ref[...]).astype(hm_ref.dtype)


def _proj_in(x2, g, wm, we):
    m = x2.shape[0]
    tm, tn = min(PROJ_TM, m), PROJ_TN
    return pl.pallas_call(
        _proj_kernel,
        grid=(m // tm, N_MAIN // tn),
        in_specs=[
            pl.BlockSpec((tm, D_MODEL), lambda i, j: (i, 0)),
            pl.BlockSpec((1, D_MODEL), lambda i, j: (0, 0)),
            pl.BlockSpec((D_MODEL, tn), lambda i, j: (0, j)),
            pl.BlockSpec((D_MODEL, N_EXT), lambda i, j: (0, 0)),
        ],
        out_specs=[
            pl.BlockSpec((tm, tn), lambda i, j: (i, j)),
            pl.BlockSpec((tm, N_EXT), lambda i, j: (i, 0)),
        ],
        out_shape=[
            jax.ShapeDtypeStruct((m, N_MAIN), BF16),
            jax.ShapeDtypeStruct((m, N_EXT), F32),
        ],
        scratch_shapes=[pltpu.VMEM((tm, D_MODEL), BF16)],
        compiler_params=pltpu.CompilerParams(
            dimension_semantics=("parallel", "arbitrary"), vmem_limit_bytes=VMEM_LIMIT),
        name="proj_in",
    )(x2, g, wm, we)


def _rwkv_kernel(has_vres, *refs):
    if has_vres:
        (r_ref, k_ref, v_ref, z_ref, wa_ref, vl_ref, vf_ref, pv_ref, w2_ref, vw2_ref,
         o_ref, state_ref, cr_ref, ck_ref, cv_ref, cwa_ref) = refs
    else:
        (r_ref, k_ref, v_ref, z_ref, wa_ref, pv_ref, w2_ref,
         o_ref, vfo_ref, state_ref, cr_ref, ck_ref, cv_ref, cwa_ref) = refs

    @pl.when(pl.program_id(1) == 0)
    def _():
        for ref in (state_ref, cr_ref, ck_ref, cv_ref, cwa_ref):
            ref[...] = jnp.zeros_like(ref)

    row = lax.broadcasted_iota(jnp.int32, (RWKV_TBLK, 1), 0)

    def shift_mix(x_ref, c_ref, mu):
        x = x_ref[...].astype(F32)
        prev = c_ref[SUBLANES - 1:SUBLANES, :]
        xs = jnp.where(row == 0, prev, pltpu.roll(x, 1, 0))
        c_ref[...] = x[RWKV_TBLK - SUBLANES:RWKV_TBLK, :]
        return x + mu * (xs - x)

    pv = pv_ref[...]
    prow = lambda i: pv[i:i + 1]
    mu_r, mu_k, mu_v = prow(PV_MU_R), prow(PV_MU_K), prow(PV_MU_V)
    w0, a0, k_k, k_a, r_k = prow(PV_W0), prow(PV_A0), prow(PV_KK), prow(PV_KA), prow(PV_RK)
    lnx_g, lnx_b, v0 = prow(PV_LNG), prow(PV_LNB), prow(PV_V0)
    mu_wa = prow(PV_MU_WA)[:, 0:LANES]

    r = shift_mix(r_ref, cr_ref, mu_r)
    k = shift_mix(k_ref, ck_ref, mu_k)
    v = shift_mix(v_ref, cv_ref, mu_v)
    wa = shift_mix(wa_ref, cwa_ref, mu_wa)

    lane = lax.broadcasted_iota(jnp.int32, (1, LANES), 1)
    tl = jnp.where(lane < R_LORA, jnp.tanh(wa), wa).astype(BF16)
    pre = _dot(tl, w2_ref[...])
    lw = -math.exp(-0.5) * _sigmoid(w0 + pre[:, :MIX_W])
    a = _sigmoid(a0 + pre[:, MIX_W:])

    if has_vres:
        gate = _sigmoid(v0 + _dot(vl_ref[...].astype(BF16), vw2_ref[...]))
        v = v + (vf_ref[...].astype(F32) - v) * gate
    else:
        vfo_ref[...] = v.astype(vfo_ref.dtype)

    ti = lax.broadcasted_iota(jnp.int32, (RWKV_TBLK, RWKV_TBLK), 0)
    si = lax.broadcasted_iota(jnp.int32, (RWKV_TBLK, RWKV_TBLK), 1)
    same_chunk = jnp.right_shift(ti, CHUNK_LOG2) == jnp.right_shift(si, CHUNK_LOG2)
    tri = jnp.where(same_chunk & (si <= ti), 1.0, 0.0).astype(BF16)
    cum = _split_dot(tri, lw, 2)
    ecum = jnp.exp(cum)
    ecum_prev = jnp.exp(cum - lw)
    einv = jnp.exp(-cum)
    chunks = range(RWKV_TBLK // CHUNK)
    rcs = [slice(c * CHUNK, (c + 1) * CHUNK) for c in chunks]
    cls = [cum[(c + 1) * CHUNK - 1:(c + 1) * CHUNK] for c in chunks]
    g_cs = [jnp.exp(cl) for cl in cls]

    groups = range(N_GROUPS)
    gls = [slice(g * GROUP_W, (g + 1) * GROUP_W) for g in groups]
    units = [(c, g) for c in chunks for g in groups]
    t_i = lax.broadcasted_iota(jnp.int32, (CHUNK, GROUP_W), 0)
    l_i = lax.broadcasted_iota(jnp.int32, (CHUNK, GROUP_W), 1)
    s_i = jnp.bitwise_and(l_i, RWKV_HEAD_W - 1)
    strict = t_i > s_i
    incl = t_i >= s_i
    eye = jnp.where(t_i == s_i, 1.0, 0.0).astype(F32)
    head_of_lane = jnp.right_shift(l_i, CHUNK_LOG2)
    head_masks = [jnp.where(head_of_lane == h, 1.0, 0.0).astype(BF16)
                  for h in range(GROUP_W // RWKV_HEAD_W)]
    bi = lax.broadcasted_iota(jnp.int32, (GROUP_W, GROUP_W), 0)
    bj = lax.broadcasted_iota(jnp.int32, (GROUP_W, GROUP_W), 1)
    same_head = jnp.right_shift(bi, CHUNK_LOG2) == jnp.right_shift(bj, CHUNK_LOG2)
    seg = jnp.where(same_head, 1.0, 0.0).astype(BF16)

    def stack4(x):
        xb = x.astype(BF16)
        return jnp.concatenate([xb * m for m in head_masks], axis=0)

    def segsum(x):
        return _dot(x.astype(BF16), seg)

    kk_raw = k * k_k
    k2 = k * (1.0 + (a - 1.0) * k_a)
    sums2 = [segsum(jnp.concatenate([kk_raw[:, gl] * kk_raw[:, gl],
                                     r[:, gl] * k2[:, gl] * r_k[:, gl]], axis=0)) for gl in gls]
    lhss, rhss, bks, v_sts = {}, {}, {}, {}
    for c, g in units:
        rc, gl = rcs[c], gls[g]
        kk = kk_raw[rc, gl] * lax.rsqrt(jnp.maximum(sums2[g][rc], KK_NORM_FLOOR_SQ))
        kka = kk * a[rc, gl]
        at = -kk * ecum_prev[rc, gl]
        bt = kka * einv[rc, gl]
        kt = k2[rc, gl] * einv[rc, gl]
        rt = r[rc, gl] * ecum[rc, gl]
        lhss[c, g] = jnp.concatenate([at, rt], axis=0).astype(BF16)
        rhss[c, g] = jnp.concatenate([stack4(bt), stack4(kt)], axis=0)
        bks[c, g] = jnp.concatenate([bt, kt], axis=0).astype(BF16)
        v_sts[c, g] = stack4(v[rc, gl])

    gs = {u: _dot_nt(lhss[u], rhss[u]) for u in units}
    n_abs = {u: jnp.where(strict, gs[u][0:CHUNK, 0:GROUP_W], 0.0) for u in units}
    g_ks = {u: jnp.concatenate([jnp.where(strict, gs[u][0:CHUNK, GROUP_W:], 0.0),
                                jnp.where(incl, gs[u][CHUNK:, GROUP_W:], 0.0)],
                               axis=0).astype(BF16) for u in units}
    g_rbs = {u: jnp.where(incl, gs[u][CHUNK:, 0:GROUP_W], 0.0).astype(BF16) for u in units}
    gkvs = {u: _dot(g_ks[u], v_sts[u]) for u in units}

    psts = {u: stack4(n_abs[u]) for u in units}
    pws = {u: _dot(n_abs[u].astype(BF16), psts[u]) for u in units}
    sks = {u: eye + n_abs[u] for u in units}
    for _ in range(4):
        psts = {u: stack4(pws[u]) for u in units}
        prods = {u: _dot(jnp.concatenate([sks[u], pws[u]], axis=0).astype(BF16), psts[u])
                 for u in units}
        sks = {u: sks[u] + prods[u][0:CHUNK] for u in units}
        pws = {u: prods[u][CHUNK:] for u in units}
    psts = {u: stack4(pws[u]) for u in units}
    tinvs = {u: (sks[u] + _dot(sks[u].astype(BF16), psts[u])).astype(BF16) for u in units}

    states = [state_ref[g] for g in groups]
    ys = {}
    for c in chunks:
        ahs = [_dot_nt(lhss[c, g], states[g].astype(BF16)) for g in groups]
        us = [_dot(tinvs[c, g], stack4(ahs[g][0:CHUNK] + gkvs[c, g][0:CHUNK])) for g in groups]
        for g in groups:
            ys[c, g] = (ahs[g][CHUNK:] + gkvs[c, g][CHUNK:]
                        + _dot(g_rbs[c, g], stack4(us[g])))
        uvs = [jnp.concatenate([us[g], v[rcs[c], gls[g]]], axis=0).astype(BF16) for g in groups]
        states = [(states[g] + jnp.where(same_head, _dot_tn(uvs[g], bks[c, g]), 0.0))
                  * g_cs[c][:, gls[g]] for g in groups]
    state_ref[...] = jnp.stack(states, axis=0)

    y_rows = [jnp.concatenate([ys[c, g] for c in chunks], axis=0) for g in groups]
    means = [segsum(y) * (1.0 / RWKV_HEAD_W) for y in y_rows]
    ds = [y_rows[g] - means[g] for g in groups]
    variances = [segsum(d * d) * (1.0 / RWKV_HEAD_W) for d in ds]
    outs = []
    for g, gl in enumerate(gls):
        yn = ds[g] * lax.rsqrt(variances[g] + GN_EPS) * lnx_g[:, gl] + lnx_b[:, gl]
        bonus = sums2[g][RWKV_TBLK:] * v[:, gl]
        outs.append((yn + bonus) * _silu(z_ref[:, gl].astype(F32)))
    o_ref[...] = jnp.concatenate(outs, axis=1).astype(o_ref.dtype)


def _rwkv(hm, he, vfirst, pvec, w2, vw2, batch, seq):
    nt = seq // RWKV_TBLK
    has_vres = vfirst is not None
    tok = lambda b, t: b * nt + t
    slab = lambda s: pl.BlockSpec((RWKV_TBLK, MIX_W), lambda b, t: (tok(b, t), s))
    ext = lambda s: pl.BlockSpec((RWKV_TBLK, LANES), lambda b, t: (tok(b, t), s))
    full = lambda a: pl.BlockSpec(a.shape, lambda b, t: (0,) * a.ndim)
    act = pl.BlockSpec((RWKV_TBLK, MIX_W), lambda b, t: (tok(b, t), 0))
    in_specs = [slab(SLAB_R), slab(SLAB_K), slab(SLAB_V), slab(SLAB_ZA), ext(0)]
    args = [hm, hm, hm, hm, he]
    if has_vres:
        in_specs += [ext(1), act, full(pvec), full(w2), full(vw2)]
        args += [he, vfirst, pvec, w2, vw2]
        out_specs = act
        out_shape = jax.ShapeDtypeStruct((batch * seq, MIX_W), BF16)
    else:
        in_specs += [full(pvec), full(w2)]
        args += [pvec, w2]
        out_specs = [act, act]
        out_shape = [jax.ShapeDtypeStruct((batch * seq, MIX_W), BF16),
                     jax.ShapeDtypeStruct((batch * seq, MIX_W), BF16)]
    return pl.pallas_call(
        functools.partial(_rwkv_kernel, has_vres),
        grid=(batch, nt),
        in_specs=in_specs,
        out_specs=out_specs,
        out_shape=out_shape,
        scratch_shapes=[
            pltpu.VMEM((N_GROUPS, GROUP_W, GROUP_W), F32),
            pltpu.VMEM((SUBLANES, MIX_W), F32),
            pltpu.VMEM((SUBLANES, MIX_W), F32),
            pltpu.VMEM((SUBLANES, MIX_W), F32),
            pltpu.VMEM((SUBLANES, LANES), F32),
        ],
        compiler_params=pltpu.CompilerParams(
            dimension_semantics=("parallel", "arbitrary"), vmem_limit_bytes=VMEM_LIMIT),
        name="rwkv",
    )(*args)


def _sgu_rows(n_rows, u_ref, v_ref, z_ref, lng_ref, lnb_ref, ws_ref, bias_ref):
    v = v_ref[...].astype(F32)
    mean = jnp.mean(v, axis=-1, keepdims=True)
    d = v - mean
    var = jnp.mean(d * d, axis=-1, keepdims=True)
    vn = (d * lax.rsqrt(var + EPS) * lng_ref[...] + lnb_ref[...]).astype(BF16)
    ti = lax.broadcasted_iota(jnp.int32, (SGU_BLOCK, SGU_BLOCK), 0)
    si = lax.broadcasted_iota(jnp.int32, (SGU_BLOCK, SGU_BLOCK), 1)
    causal = (ti >= CHUNK) | (si < CHUNK)
    cols = []
    for g in range(SGU_GROUPS):
        sl = slice(g * LANES, (g + 1) * LANES)
        ws = jnp.where(causal, ws_ref[g], 0.0).astype(BF16)
        blocks = []
        for b in range(n_rows // SGU_BLOCK):
            rows = slice(b * SGU_BLOCK, (b + 1) * SGU_BLOCK)
            mixed = _dot(ws, vn[rows, sl]) + bias_ref[:, sl]
            blocks.append(u_ref[rows, sl].astype(F32) * mixed * _silu(z_ref[rows, sl].astype(F32)))
        cols.append(jnp.concatenate(blocks, axis=0))
    return jnp.concatenate(cols, axis=1)


def _hgrn_kernel(n_sub, q_ref, f_ref, i_ref, z_ref, lb_ref, g_ref, o_ref, state_ref, kx_ref, fg_ref,
                 ic_ref):
    @pl.when(pl.program_id(2) == 0)
    def _():
        state_ref[...] = jnp.zeros_like(state_ref)

    for sub in range(n_sub):
        _hgrn_block(slice(sub * HGRN_SUB, (sub + 1) * HGRN_SUB), q_ref, f_ref, i_ref, z_ref,
                    lb_ref, g_ref, o_ref, state_ref, kx_ref, fg_ref, ic_ref)


def _hgrn_block(rows, q_ref, f_ref, i_ref, z_ref, lb_ref, g_ref, o_ref, state_ref, kx_ref, fg_ref,
                ic_ref):
    lb = lb_ref[...]
    gi = lax.broadcasted_iota(jnp.int32, (HGRN_SUB, HGRN_SUB), 0)
    gj = lax.broadcasted_iota(jnp.int32, (HGRN_SUB, HGRN_SUB), 1)
    same_step = jnp.right_shift(gi, HGRN_STEP_LOG2) == jnp.right_shift(gj, HGRN_STEP_LOG2)
    tri = jnp.where(same_step & (gj <= gi), 1.0, 0.0).astype(BF16)
    ones = jnp.ones((HGRN_HEAD, HGRN_HEAD), BF16)
    half = HGRN_STEP // 2
    trow = lax.broadcasted_iota(jnp.int32, (half, 1), 0)

    q = _silu(q_ref[rows, :].astype(F32))
    fg = lb + (1.0 - lb) * _sigmoid(f_ref[rows, :].astype(F32))
    kx = 1.0 - fg
    ic = i_ref[rows, :].astype(F32)
    kx_ref[...] = kx
    fg_ref[...] = fg
    ic_ref[...] = ic

    def row_bcast(ref, r):
        return jnp.broadcast_to(ref[r:r + 1, :], (half, HGRN_HEAD))

    bcs = _split_dot(tri, jnp.log(fg), 2)
    steps = range(HGRN_SUB // HGRN_STEP)
    rss = [slice(j * HGRN_STEP, (j + 1) * HGRN_STEP) for j in steps]
    bls = [bcs[rs][HGRN_STEP - 1:HGRN_STEP] for rs in rss]
    atts_lo, atts_hi = [], []
    for j, rs in enumerate(rss):
        qs, base = q[rs], j * HGRN_STEP
        q_lo, q_hi = qs[0:half], qs[half:HGRN_STEP]
        p_lo, p_hi = [None] * half, [None] * HGRN_STEP
        lo = hi = None
        for s in range(HGRN_STEP - 1, -1, -1):
            nxt = row_bcast(fg_ref, base + s + 1) if s + 1 < HGRN_STEP else None
            k_s = row_bcast(kx_ref, base + s)
            if s >= half:
                prev = 0.0 if hi is None else hi * nxt
                hi = jnp.where(trow == s - half, q_hi, prev)
            else:
                hi = hi * nxt
                prev = 0.0 if lo is None else lo * nxt
                lo = jnp.where(trow == s, q_lo, prev)
                p_lo[s] = lo * k_s
            p_hi[s] = hi * k_s
        stack = jnp.concatenate(p_lo + p_hi, axis=0).astype(BF16)
        att = _dot(stack, ones)
        atts_lo.append(att[0:half * half])
        atts_hi.append(att[half * half:])
    incs = [_dot_tn(ic[rs].astype(BF16), (kx[rs] * jnp.exp(bl - bcs[rs])).astype(BF16))
            for rs, bl in zip(rss, bls)]
    qb = (q * jnp.exp(bcs)).astype(BF16)
    st = state_ref[...]
    o_inters = []
    for j in steps:
        o_inters.append(_dot_nt(qb[rss[j]], st.astype(BF16)))
        st = st * jnp.exp(bls[j]) + incs[j]
    state_ref[...] = st
    outs = []
    for j, rs in enumerate(rss):
        a_lo, a_hi, base = atts_lo[j], atts_hi[j], j * HGRN_STEP
        o_lo = a_lo[0:half] * row_bcast(ic_ref, base)
        for s in range(1, half):
            o_lo = o_lo + a_lo[s * half:(s + 1) * half] * row_bcast(ic_ref, base + s)
        o_hi = a_hi[0:half] * row_bcast(ic_ref, base)
        for s in range(1, HGRN_STEP):
            o_hi = o_hi + a_hi[s * half:(s + 1) * half] * row_bcast(ic_ref, base + s)
        outs.append(o_inters[j] + jnp.concatenate([o_lo, o_hi], axis=0))
    o = jnp.concatenate(outs, axis=0)
    o = o * lax.rsqrt(jnp.mean(o * o, axis=-1, keepdims=True) + EPS) * g_ref[...]
    o_ref[rows, :] = (o * _silu(z_ref[rows, :].astype(F32))).astype(o_ref.dtype)


def _hgrn(hm, lb, g_norm, batch, seq):
    tblk = min(HGRN_TBLK, seq)
    nt = seq // tblk
    col = lambda s: pl.BlockSpec(
        (tblk, HGRN_HEAD), lambda b, h, t: (b * nt + t, s * HGRN_HEADS + h))
    vec = pl.BlockSpec((1, HGRN_HEAD), lambda b, h, t: (0, h))
    return pl.pallas_call(
        functools.partial(_hgrn_kernel, tblk // HGRN_SUB),
        grid=(batch, HGRN_HEADS, nt),
        in_specs=[col(SLAB_QC), col(SLAB_FC), col(SLAB_IC), col(SLAB_ZC), vec, vec],
        out_specs=pl.BlockSpec((tblk, HGRN_HEAD), lambda b, h, t: (b * nt + t, h)),
        out_shape=jax.ShapeDtypeStruct((batch * seq, MIX_W), BF16),
        scratch_shapes=[pltpu.VMEM((HGRN_HEAD, HGRN_HEAD), F32),
                        pltpu.VMEM((HGRN_SUB, HGRN_HEAD), F32),
                        pltpu.VMEM((HGRN_SUB, HGRN_HEAD), F32),
                        pltpu.VMEM((HGRN_SUB, HGRN_HEAD), F32)],
        compiler_params=pltpu.CompilerParams(
            dimension_semantics=("parallel", "parallel", "arbitrary"),
            vmem_limit_bytes=VMEM_LIMIT),
        name="hgrn",
    )(hm, hm, hm, hm, lb, g_norm)


def _merge_kernel(final, *refs):
    sgu_refs, refs = refs[:7], refs[7:]
    if final:
        oa, oc, g0, g1, g2, x_ref, bm_ref, wb_ref, wo_ref, fg_ref, out_ref = refs
    else:
        oa, oc, g0, g1, g2, x_ref, bm_ref, wb_ref, wo_ref, out_ref = refs
    ob = _sgu_rows(x_ref.shape[0], *sgu_refs).astype(BF16)
    y = None
    for n, (o_val, gl_ref) in enumerate(((oa[...], g0), (ob, g1), (oc[...], g2))):
        proj = _dot(o_val, wb_ref[n])
        term = _sigmoid(gl_ref[...].astype(F32) + bm_ref[n:n + 1, :]) * proj
        y = term if y is None else y + term
    xo = x_ref[...] + _dot(y.astype(BF16), wo_ref[...])
    if final:
        ms = jnp.mean(xo * xo, axis=-1, keepdims=True)
        xo = xo * lax.rsqrt(ms + EPS) * fg_ref[...]
    out_ref[...] = xo


def _merge(oa, oc, hm, x2, sgu_params, bm, wb, wo, final_g):
    m = x2.shape[0]
    tm = min(MERGE_TM, m)
    final = final_g is not None
    act = pl.BlockSpec((tm, D_MODEL), lambda i: (i, 0))
    gl = lambda n: pl.BlockSpec((tm, D_MODEL), lambda i: (i, SLAB_GL + n))
    full = lambda a: pl.BlockSpec(a.shape, lambda i: (0,) * a.ndim)
    slab = lambda s_: pl.BlockSpec((tm, MIX_W), lambda i: (i, s_))
    in_specs = ([slab(SLAB_UB), slab(SLAB_VB), slab(SLAB_ZB)] + [full(p) for p in sgu_params]
                + [act, act, gl(0), gl(1), gl(2), act, full(bm), full(wb), full(wo)])
    args = [hm, hm, hm, *sgu_params, oa, oc, hm, hm, hm, x2, bm, wb, wo]
    if final:
        in_specs.append(full(final_g))
        args.append(final_g)
    return pl.pallas_call(
        functools.partial(_merge_kernel, final),
        grid=(m // tm,),
        in_specs=in_specs,
        out_specs=act,
        out_shape=jax.ShapeDtypeStruct((m, D_MODEL), F32),
        compiler_params=pltpu.CompilerParams(
            dimension_semantics=("parallel",), vmem_limit_bytes=VMEM_LIMIT),
        name="merge",
    )(*args)


def kernel(x, norm_g, w_in, b_merge, mu_shift, w0, w_w2, a0, a_w2, v0, v_w1, v_w2, k_k, k_a, r_k,
           lnx_g, lnx_b, sgu_ln_g, sgu_ln_b, sgu_w, sgu_b, lb_logits, hgrn_g, w_branch, w_out,
           final_g):
    batch, seq, _ = x.shape
    depth = w_in.shape[0]
    assert seq % min(HGRN_TBLK, seq) == 0 and seq % HGRN_SUB == 0 and x.shape[2] == D_MODEL
    assert seq % RWKV_TBLK == 0 and MERGE_TM % SGU_BLOCK == 0 and seq % SGU_BLOCK == 0
    n_shift = 3 * MIX_W + 2 * R_LORA

    lb_all = jnp.cumsum(jax.nn.softmax(lb_logits.astype(F32), axis=0), axis=0)
    lb_all = lb_all - lb_all[0]

    x2 = x.reshape(batch * seq, D_MODEL)
    vfirst = None
    for l in range(depth):
        w = w_in[l]
        wm = jnp.concatenate([w[:, :3 * MIX_W], w[:, n_shift:]], axis=1).astype(BF16)
        ext = [w[:, 3 * MIX_W:n_shift]]
        if l > 0:
            ext.append(v_w1[l - 1])
        we = jnp.concatenate(ext, axis=1)
        we = jnp.pad(we, ((0, 0), (0, N_EXT - we.shape[1]))).astype(BF16)

        zrow = jnp.zeros((MIX_W,), F32)
        mu = mu_shift[l]
        rows = [mu[0:MIX_W], mu[MIX_W:2 * MIX_W], mu[2 * MIX_W:3 * MIX_W], w0[l], a0[l], k_k[l],
                k_a[l], r_k[l], lnx_g[l], lnx_b[l],
                v0[l - 1] if l > 0 else zrow,
                jnp.pad(mu[3 * MIX_W:n_shift], (0, MIX_W - 2 * R_LORA))]
        assert len(rows) == PV_ROWS
        pvec = jnp.stack(rows + [zrow] * (2 * SUBLANES - PV_ROWS), axis=0)
        zl = jnp.zeros((R_LORA, MIX_W), F32)
        w2 = jnp.concatenate(
            [jnp.concatenate([w_w2[l], zl], axis=0), jnp.concatenate([zl, a_w2[l]], axis=0)],
            axis=1).astype(BF16)

        hm, he = _proj_in(x2, norm_g[l][None, :], wm, we)

        if l == 0:
            oa, vfirst = _rwkv(hm, he, None, pvec, w2, None, batch, seq)
        else:
            vw2 = jnp.pad(v_w2[l - 1], ((0, LANES - R_V), (0, 0))).astype(BF16)
            oa = _rwkv(hm, he, vfirst, pvec, w2, vw2, batch, seq)

        bias = jnp.repeat(sgu_b[l].T, LANES, axis=1)
        sgu_params = (sgu_ln_g[l][None, :], sgu_ln_b[l][None, :], sgu_w[l], bias)
        oc = _hgrn(hm, lb_all[l][None, :], hgrn_g[l][None, :], batch, seq)

        bm = jnp.pad(b_merge[l], ((0, SUBLANES - N_BRANCH), (0, 0)))
        x2 = _merge(oa, oc, hm, x2, sgu_params, bm, w_branch[l].astype(BF16), w_out[l].astype(BF16),
                    final_g[None, :] if l == depth - 1 else None)
    return x2.reshape(batch, seq, D_MODEL)
```

```python
import functools
import math

import jax
import jax.numpy as jnp
from jax import lax
from jax.experimental import pallas as pl
from jax.experimental.pallas import tpu as pltpu

F32 = jnp.float32
BF16 = jnp.bfloat16

D_MODEL = 1024
MIX_W = 1024
N_BRANCH = 3
CHUNK = 64
CHUNK_LOG2 = 6
RWKV_TBLK = 512
RWKV_HEAD_W = 64
R_LORA = 64
R_V = 32
SGU_BLOCK = 128
SGU_GROUPS = 8
HGRN_HEAD = 128
HGRN_HEADS = MIX_W // HGRN_HEAD
HGRN_STEP = 16
HGRN_STEP_LOG2 = 4
HGRN_SUB = 512
HGRN_TBLK = 4096
EPS = 1e-6
GN_EPS = 64e-5
LANES = 128
SUBLANES = 8
GROUP_W = 256
N_GROUPS = MIX_W // GROUP_W
N_MAIN = 14 * 1024
N_EXT = 256
VMEM_LIMIT = 48 * 1024 * 1024
PROJ_TM, PROJ_TN = 2048, 1024
MERGE_TM = 512
KK_NORM_FLOOR_SQ = 1e-24
(PV_MU_R, PV_MU_K, PV_MU_V, PV_W0, PV_A0, PV_KK, PV_KA, PV_RK, PV_LNG, PV_LNB, PV_V0,
 PV_MU_WA, PV_ROWS) = range(12 + 1)

SLAB_R, SLAB_K, SLAB_V, SLAB_ZA, SLAB_UB, SLAB_VB, SLAB_ZB = 0, 1, 2, 3, 4, 5, 6
SLAB_QC, SLAB_FC, SLAB_IC, SLAB_ZC, SLAB_GL = 7, 8, 9, 10, 11


def _dot(a, b):
    return jnp.dot(a, b, preferred_element_type=F32)


def _dot_nt(a, b):
    return lax.dot_general(a, b, (((1,), (1,)), ((), ())), preferred_element_type=F32)


def _dot_tn(a, b):
    return lax.dot_general(a, b, (((0,), (0,)), ((), ())), preferred_element_type=F32)


def _split_dot(sel, x, terms):
    acc = None
    rem = x
    for _ in range(terms):
        part = rem.astype(BF16)
        d = _dot(sel, part)
        acc = d if acc is None else acc + d
        rem = rem - part.astype(F32)
    return acc


def _sigmoid(x):
    return jax.nn.sigmoid(x)


def _silu(x):
    return x * jax.nn.sigmoid(x)


def _proj_kernel(x_ref, g_ref, wm_ref, we_ref, hm_ref, he_ref, xn_ref):
    @pl.when(pl.program_id(1) == 0)
    def _():
        x = x_ref[...]
        ms = jnp.mean(x * x, axis=-1, keepdims=True)
        xn = (x * lax.rsqrt(ms + EPS) * g_ref[...]).astype(BF16)
        xn_ref[...] = xn
        he_ref[...] = _dot(xn, we_ref[...])

    hm_ref[...] = _dot(xn_ref[...], wm_ref[...]).astype(hm_ref.dtype)


def _proj_in(x2, g, wm_all, layer, we):
    m = x2.shape[0]
    tm, tn = min(PROJ_TM, m), PROJ_TN
    return pl.pallas_call(
        _proj_kernel,
        grid=(m // tm, N_MAIN // tn),
        in_specs=[
            pl.BlockSpec((tm, D_MODEL), lambda i, j: (i, 0)),
            pl.BlockSpec((1, D_MODEL), lambda i, j: (0, 0)),
            pl.BlockSpec((None, D_MODEL, tn), lambda i, j: (layer, 0, j)),
            pl.BlockSpec((D_MODEL, N_EXT), lambda i, j: (0, 0)),
        ],
        out_specs=[
            pl.BlockSpec((tm, tn), lambda i, j: (i, j)),
            pl.BlockSpec((tm, N_EXT), lambda i, j: (i, 0)),
        ],
        out_shape=[
            jax.ShapeDtypeStruct((m, N_MAIN), BF16),
            jax.ShapeDtypeStruct((m, N_EXT), F32),
        ],
        scratch_shapes=[pltpu.VMEM((tm, D_MODEL), BF16)],
        compiler_params=pltpu.CompilerParams(
            dimension_semantics=("parallel", "arbitrary"), vmem_limit_bytes=VMEM_LIMIT),
        name="proj_in",
    )(x2, g, wm_all, we)


def _rwkv_kernel(has_vres, *refs):
    if has_vres:
        (r_ref, k_ref, v_ref, z_ref, wa_ref, vl_ref, vf_ref, pv_ref, w2_ref, vw2_ref,
         o_ref, state_ref, cr_ref, ck_ref, cv_ref, cwa_ref) = refs
    else:
        (r_ref, k_ref, v_ref, z_ref, wa_ref, pv_ref, w2_ref,
         o_ref, vfo_ref, state_ref, cr_ref, ck_ref, cv_ref, cwa_ref) = refs

    @pl.when(pl.program_id(1) == 0)
    def _():
        for ref in (state_ref, cr_ref, ck_ref, cv_ref, cwa_ref):
            ref[...] = jnp.zeros_like(ref)

    row = lax.broadcasted_iota(jnp.int32, (RWKV_TBLK, 1), 0)

    def shift_mix(x_ref, c_ref, mu):
        x = x_ref[...].astype(F32)
        prev = c_ref[SUBLANES - 1:SUBLANES, :]
        xs = jnp.where(row == 0, prev, pltpu.roll(x, 1, 0))
        c_ref[...] = x[RWKV_TBLK - SUBLANES:RWKV_TBLK, :]
        return x + mu * (xs - x)

    pv = pv_ref[...]
    prow = lambda i: pv[i:i + 1]
    mu_r, mu_k, mu_v = prow(PV_MU_R), prow(PV_MU_K), prow(PV_MU_V)
    w0, a0, k_k, k_a, r_k = prow(PV_W0), prow(PV_A0), prow(PV_KK), prow(PV_KA), prow(PV_RK)
    lnx_g, lnx_b, v0 = prow(PV_LNG), prow(PV_LNB), prow(PV_V0)
    mu_wa = prow(PV_MU_WA)[:, 0:LANES]

    r = shift_mix(r_ref, cr_ref, mu_r)
    k = shift_mix(k_ref, ck_ref, mu_k)
    v = shift_mix(v_ref, cv_ref, mu_v)
    wa = shift_mix(wa_ref, cwa_ref, mu_wa)

    lane = lax.broadcasted_iota(jnp.int32, (1, LANES), 1)
    tl = jnp.where(lane < R_LORA, jnp.tanh(wa), wa).astype(BF16)
    pre = _dot(tl, w2_ref[...])
    lw = -math.exp(-0.5) * _sigmoid(w0 + pre[:, :MIX_W])
    a = _sigmoid(a0 + pre[:, MIX_W:])

    if has_vres:
        gate = _sigmoid(v0 + _dot(vl_ref[...].astype(BF16), vw2_ref[...]))
        v = v + (vf_ref[...] - v) * gate
    else:
        vfo_ref[...] = v

    ti = lax.broadcasted_iota(jnp.int32, (RWKV_TBLK, RWKV_TBLK), 0)
    si = lax.broadcasted_iota(jnp.int32, (RWKV_TBLK, RWKV_TBLK), 1)
    same_chunk = jnp.right_shift(ti, CHUNK_LOG2) == jnp.right_shift(si, CHUNK_LOG2)
    tri = jnp.where(same_chunk & (si <= ti), 1.0, 0.0).astype(BF16)
    cum = _split_dot(tri, lw, 2)
    ecum = jnp.exp(cum)
    ecum_prev = jnp.exp(cum - lw)
    einv = jnp.exp(-cum)
    chunks = range(RWKV_TBLK // CHUNK)
    rcs = [slice(c * CHUNK, (c + 1) * CHUNK) for c in chunks]
    cls = [cum[(c + 1) * CHUNK - 1:(c + 1) * CHUNK] for c in chunks]
    g_cs = [jnp.exp(cl) for cl in cls]

    groups = range(N_GROUPS)
    gls = [slice(g * GROUP_W, (g + 1) * GROUP_W) for g in groups]
    units = [(c, g) for c in chunks for g in groups]
    t_i = lax.broadcasted_iota(jnp.int32, (CHUNK, GROUP_W), 0)
    l_i = lax.broadcasted_iota(jnp.int32, (CHUNK, GROUP_W), 1)
    s_i = jnp.bitwise_and(l_i, RWKV_HEAD_W - 1)
    strict = t_i > s_i
    incl = t_i >= s_i
    eye = jnp.where(t_i == s_i, 1.0, 0.0).astype(F32)
    head_of_lane = jnp.right_shift(l_i, CHUNK_LOG2)
    head_masks = [jnp.where(head_of_lane == h, 1.0, 0.0).astype(BF16)
                  for h in range(GROUP_W // RWKV_HEAD_W)]
    bi = lax.broadcasted_iota(jnp.int32, (GROUP_W, GROUP_W), 0)
    bj = lax.broadcasted_iota(jnp.int32, (GROUP_W, GROUP_W), 1)
    same_head = jnp.right_shift(bi, CHUNK_LOG2) == jnp.right_shift(bj, CHUNK_LOG2)
    seg = jnp.where(same_head, 1.0, 0.0).astype(BF16)

    def stack4(x):
        xb = x.astype(BF16)
        return jnp.concatenate([xb * m for m in head_masks], axis=0)

    def segsum(x):
        return _dot(x.astype(BF16), seg)

    kk_raw = k * k_k
    k2 = k * (1.0 + (a - 1.0) * k_a)
    sums2 = [segsum(jnp.concatenate([kk_raw[:, gl] * kk_raw[:, gl],
                                     r[:, gl] * k2[:, gl] * r_k[:, gl]], axis=0)) for gl in gls]
    lhss, rhss, bks, v_sts = {}, {}, {}, {}
    for c, g in units:
        rc, gl = rcs[c], gls[g]
        kk = kk_raw[rc, gl] * lax.rsqrt(jnp.maximum(sums2[g][rc], KK_NORM_FLOOR_SQ))
        kka = kk * a[rc, gl]
        at = -kk * ecum_prev[rc, gl]
        bt = kka * einv[rc, gl]
        kt = k2[rc, gl] * einv[rc, gl]
        rt = r[rc, gl] * ecum[rc, gl]
        lhss[c, g] = jnp.concatenate([at, rt], axis=0).astype(BF16)
        rhss[c, g] = jnp.concatenate([stack4(bt), stack4(kt)], axis=0)
        bks[c, g] = jnp.concatenate([bt, kt], axis=0).astype(BF16)
        v_sts[c, g] = stack4(v[rc, gl])

    gs = {u: _dot_nt(lhss[u], rhss[u]) for u in units}
    n_abs = {u: jnp.where(strict, gs[u][0:CHUNK, 0:GROUP_W], 0.0) for u in units}
    g_ks = {u: jnp.concatenate([jnp.where(strict, gs[u][0:CHUNK, GROUP_W:], 0.0),
                                jnp.where(incl, gs[u][CHUNK:, GROUP_W:], 0.0)],
                               axis=0).astype(BF16) for u in units}
    g_rbs = {u: jnp.where(incl, gs[u][CHUNK:, 0:GROUP_W], 0.0).astype(BF16) for u in units}
    gkvs = {u: _dot(g_ks[u], v_sts[u]) for u in units}

    psts = {u: stack4(n_abs[u]) for u in units}
    pws = {u: _dot(n_abs[u].astype(BF16), psts[u]) for u in units}
    sks = {u: eye + n_abs[u] for u in units}
    for _ in range(4):
        psts = {u: stack4(pws[u]) for u in units}
        prods = {u: _dot(jnp.concatenate([sks[u], pws[u]], axis=0).astype(BF16), psts[u])
                 for u in units}
        sks = {u: sks[u] + prods[u][0:CHUNK] for u in units}
        pws = {u: prods[u][CHUNK:] for u in units}
    psts = {u: stack4(pws[u]) for u in units}
    tinvs = {u: (sks[u] + _dot(sks[u].astype(BF16), psts[u])).astype(BF16) for u in units}

    states = [state_ref[g] for g in groups]
    ys = {}
    for c in chunks:
        ahs = [_dot_nt(lhss[c, g], states[g].astype(BF16)) for g in groups]
        us = [_dot(tinvs[c, g], stack4(ahs[g][0:CHUNK] + gkvs[c, g][0:CHUNK])) for g in groups]
        for g in groups:
            ys[c, g] = (ahs[g][CHUNK:] + gkvs[c, g][CHUNK:]
                        + _dot(g_rbs[c, g], stack4(us[g])))
        uvs = [jnp.concatenate([us[g], v[rcs[c], gls[g]]], axis=0).astype(BF16) for g in groups]
        states = [(states[g] + jnp.where(same_head, _dot_tn(uvs[g], bks[c, g]), 0.0))
                  * g_cs[c][:, gls[g]] for g in groups]
    state_ref[...] = jnp.stack(states, axis=0)

    y_rows = [jnp.concatenate([ys[c, g] for c in chunks], axis=0) for g in groups]
    means = [segsum(y) * (1.0 / RWKV_HEAD_W) for y in y_rows]
    ds = [y_rows[g] - means[g] for g in groups]
    variances = [segsum(d * d) * (1.0 / RWKV_HEAD_W) for d in ds]
    outs = []
    for g, gl in enumerate(gls):
        yn = ds[g] * lax.rsqrt(variances[g] + GN_EPS) * lnx_g[:, gl] + lnx_b[:, gl]
        bonus = sums2[g][RWKV_TBLK:] * v[:, gl]
        outs.append((yn + bonus) * _silu(z_ref[:, gl].astype(F32)))
    o_ref[...] = jnp.concatenate(outs, axis=1).astype(o_ref.dtype)


def _rwkv(hm, he, vfirst, pvec, w2, vw2, batch, seq):
    nt = seq // RWKV_TBLK
    has_vres = vfirst is not None
    tok = lambda b, t: b * nt + t
    slab = lambda s: pl.BlockSpec((RWKV_TBLK, MIX_W), lambda b, t: (tok(b, t), s))
    ext = lambda s: pl.BlockSpec((RWKV_TBLK, LANES), lambda b, t: (tok(b, t), s))
    full = lambda a: pl.BlockSpec(a.shape, lambda b, t: (0,) * a.ndim)
    act = pl.BlockSpec((RWKV_TBLK, MIX_W), lambda b, t: (tok(b, t), 0))
    in_specs = [slab(SLAB_R), slab(SLAB_K), slab(SLAB_V), slab(SLAB_ZA), ext(0)]
    args = [hm, hm, hm, hm, he]
    if has_vres:
        in_specs += [ext(1), act, full(pvec), full(w2), full(vw2)]
        args += [he, vfirst, pvec, w2, vw2]
        out_specs = act
        out_shape = jax.ShapeDtypeStruct((batch * seq, MIX_W), BF16)
    else:
        in_specs += [full(pvec), full(w2)]
        args += [pvec, w2]
        out_specs = [act, act]
        out_shape = [jax.ShapeDtypeStruct((batch * seq, MIX_W), BF16),
                     jax.ShapeDtypeStruct((batch * seq, MIX_W), F32)]
    return pl.pallas_call(
        functools.partial(_rwkv_kernel, has_vres),
        grid=(batch, nt),
        in_specs=in_specs,
        out_specs=out_specs,
        out_shape=out_shape,
        scratch_shapes=[
            pltpu.VMEM((N_GROUPS, GROUP_W, GROUP_W), F32),
            pltpu.VMEM((SUBLANES, MIX_W), F32),
            pltpu.VMEM((SUBLANES, MIX_W), F32),
            pltpu.VMEM((SUBLANES, MIX_W), F32),
            pltpu.VMEM((SUBLANES, LANES), F32),
        ],
        compiler_params=pltpu.CompilerParams(
            dimension_semantics=("parallel", "arbitrary"), vmem_limit_bytes=VMEM_LIMIT),
        name="rwkv",
    )(*args)


def _sgu_rows(n_rows, u_ref, v_ref, z_ref, lng_ref, lnb_ref, ws_ref, bias_ref):
    v = v_ref[...].astype(F32)
    mean = jnp.mean(v, axis=-1, keepdims=True)
    d = v - mean
    var = jnp.mean(d * d, axis=-1, keepdims=True)
    vn = (d * lax.rsqrt(var + EPS) * lng_ref[...] + lnb_ref[...]).astype(BF16)
    ti = lax.broadcasted_iota(jnp.int32, (SGU_BLOCK, SGU_BLOCK), 0)
    si = lax.broadcasted_iota(jnp.int32, (SGU_BLOCK, SGU_BLOCK), 1)
    causal = (ti >= CHUNK) | (si < CHUNK)
    cols = []
    for g in range(SGU_GROUPS):
        sl = slice(g * LANES, (g + 1) * LANES)
        ws = jnp.where(causal, ws_ref[g], 0.0).astype(BF16)
        blocks = []
        for b in range(n_rows // SGU_BLOCK):
            rows = slice(b * SGU_BLOCK, (b + 1) * SGU_BLOCK)
            mixed = _dot(ws, vn[rows, sl]) + bias_ref[:, sl]
            blocks.append(u_ref[rows, sl].astype(F32) * mixed * _silu(z_ref[rows, sl].astype(F32)))
        cols.append(jnp.concatenate(blocks, axis=0))
    return jnp.concatenate(cols, axis=1)


def _hgrn_kernel(n_sub, q_ref, f_ref, i_ref, z_ref, lb_ref, g_ref, o_ref, state_ref, kx_ref, fg_ref,
                 ic_ref):
    @pl.when(pl.program_id(2) == 0)
    def _():
        state_ref[...] = jnp.zeros_like(state_ref)

    for sub in range(n_sub):
        _hgrn_block(slice(sub * HGRN_SUB, (sub + 1) * HGRN_SUB), q_ref, f_ref, i_ref, z_ref,
                    lb_ref, g_ref, o_ref, state_ref, kx_ref, fg_ref, ic_ref)


def _hgrn_block(rows, q_ref, f_ref, i_ref, z_ref, lb_ref, g_ref, o_ref, state_ref, kx_ref, fg_ref,
                ic_ref):
    lb = lb_ref[...]
    gi = lax.broadcasted_iota(jnp.int32, (HGRN_SUB, HGRN_SUB), 0)
    gj = lax.broadcasted_iota(jnp.int32, (HGRN_SUB, HGRN_SUB), 1)
    same_step = jnp.right_shift(gi, HGRN_STEP_LOG2) == jnp.right_shift(gj, HGRN_STEP_LOG2)
    tri = jnp.where(same_step & (gj <= gi), 1.0, 0.0).astype(BF16)
    ones = jnp.ones((HGRN_HEAD, HGRN_HEAD), BF16)
    half = HGRN_STEP // 2
    trow = lax.broadcasted_iota(jnp.int32, (half, 1), 0)

    q = _silu(q_ref[rows, :].astype(F32))
    fg = lb + (1.0 - lb) * _sigmoid(f_ref[rows, :].astype(F32))
    kx = 1.0 - fg
    ic = i_ref[rows, :].astype(F32)
    kx_ref[...] = kx
    fg_ref[...] = fg
    ic_ref[...] = ic

    def row_bcast(ref, r):
        return jnp.broadcast_to(ref[r:r + 1, :], (half, HGRN_HEAD))

    bcs = _split_dot(tri, jnp.log(fg), 2)
    steps = range(HGRN_SUB // HGRN_STEP)
    rss = [slice(j * HGRN_STEP, (j + 1) * HGRN_STEP) for j in steps]
    bls = [bcs[rs][HGRN_STEP - 1:HGRN_STEP] for rs in rss]
    atts_lo, atts_hi = [], []
    for j, rs in enumerate(rss):
        qs, base = q[rs], j * HGRN_STEP
        q_lo, q_hi = qs[0:half], qs[half:HGRN_STEP]
        p_lo, p_hi = [None] * half, [None] * HGRN_STEP
        lo = hi = None
        for s in range(HGRN_STEP - 1, -1, -1):
            nxt = row_bcast(fg_ref, base + s + 1) if s + 1 < HGRN_STEP else None
            k_s = row_bcast(kx_ref, base + s)
            if s >= half:
                prev = 0.0 if hi is None else hi * nxt
                hi = jnp.where(trow == s - half, q_hi, prev)
            else:
                hi = hi * nxt
                prev = 0.0 if lo is None else lo * nxt
                lo = jnp.where(trow == s, q_lo, prev)
                p_lo[s] = lo * k_s
            p_hi[s] = hi * k_s
        stack = jnp.concatenate(p_lo + p_hi, axis=0).astype(BF16)
        att = _dot(stack, ones)
        atts_lo.append(att[0:half * half])
        atts_hi.append(att[half * half:])
    incs = [_dot_tn(ic[rs].astype(BF16), (kx[rs] * jnp.exp(bl - bcs[rs])).astype(BF16))
            for rs, bl in zip(rss, bls)]
    qb = (q * jnp.exp(bcs)).astype(BF16)
    st = state_ref[...]
    o_inters = []
    for j in steps:
        o_inters.append(_dot_nt(qb[rss[j]], st.astype(BF16)))
        st = st * jnp.exp(bls[j]) + incs[j]
    state_ref[...] = st
    outs = []
    for j, rs in enumerate(rss):
        a_lo, a_hi, base = atts_lo[j], atts_hi[j], j * HGRN_STEP
        o_lo = a_lo[0:half] * row_bcast(ic_ref, base)
        for s in range(1, half):
            o_lo = o_lo + a_lo[s * half:(s + 1) * half] * row_bcast(ic_ref, base + s)
        o_hi = a_hi[0:half] * row_bcast(ic_ref, base)
        for s in range(1, HGRN_STEP):
            o_hi = o_hi + a_hi[s * half:(s + 1) * half] * row_bcast(ic_ref, base + s)
        outs.append(o_inters[j] + jnp.concatenate([o_lo, o_hi], axis=0))
    o = jnp.concatenate(outs, axis=0)
    o = o * lax.rsqrt(jnp.mean(o * o, axis=-1, keepdims=True) + EPS) * g_ref[...]
    o_ref[rows, :] = (o * _silu(z_ref[rows, :].astype(F32))).astype(o_ref.dtype)


def _hgrn(hm, lb, g_norm, batch, seq):
    tblk = min(HGRN_TBLK, seq)
    nt = seq // tblk
    col = lambda s: pl.BlockSpec(
        (tblk, HGRN_HEAD), lambda b, h, t: (b * nt + t, s * HGRN_HEADS + h))
    vec = pl.BlockSpec((1, HGRN_HEAD), lambda b, h, t: (0, h))
    return pl.pallas_call(
        functools.partial(_hgrn_kernel, tblk // HGRN_SUB),
        grid=(batch, HGRN_HEADS, nt),
        in_specs=[col(SLAB_QC), col(SLAB_FC), col(SLAB_IC), col(SLAB_ZC), vec, vec],
        out_specs=pl.BlockSpec((tblk, HGRN_HEAD), lambda b, h, t: (b * nt + t, h)),
        out_shape=jax.ShapeDtypeStruct((batch * seq, MIX_W), BF16),
        scratch_shapes=[pltpu.VMEM((HGRN_HEAD, HGRN_HEAD), F32),
                        pltpu.VMEM((HGRN_SUB, HGRN_HEAD), F32),
                        pltpu.VMEM((HGRN_SUB, HGRN_HEAD), F32),
                        pltpu.VMEM((HGRN_SUB, HGRN_HEAD), F32)],
        compiler_params=pltpu.CompilerParams(
            dimension_semantics=("parallel", "parallel", "arbitrary"),
            vmem_limit_bytes=VMEM_LIMIT),
        name="hgrn",
    )(hm, hm, hm, hm, lb, g_norm)


def _merge_kernel(final, *refs):
    sgu_refs, refs = refs[:7], refs[7:]
    if final:
        oa, oc, g0, g1, g2, x_ref, bm_ref, wb_ref, wo_ref, fg_ref, out_ref = refs
    else:
        oa, oc, g0, g1, g2, x_ref, bm_ref, wb_ref, wo_ref, out_ref = refs
    ob = _sgu_rows(x_ref.shape[0], *sgu_refs).astype(BF16)
    y = None
    for n, (o_val, gl_ref) in enumerate(((oa[...], g0), (ob, g1), (oc[...], g2))):
        proj = _dot(o_val, wb_ref[n])
        term = _sigmoid(gl_ref[...].astype(F32) + bm_ref[n:n + 1, :]) * proj
        y = term if y is None else y + term
    xo = x_ref[...] + _dot(y.astype(BF16), wo_ref[...])
    if final:
        ms = jnp.mean(xo * xo, axis=-1, keepdims=True)
        xo = xo * lax.rsqrt(ms + EPS) * fg_ref[...]
    out_ref[...] = xo


def _merge(oa, oc, hm, x2, sgu_params, bm, wb, wo, final_g):
    m = x2.shape[0]
    tm = min(MERGE_TM, m)
    final = final_g is not None
    act = pl.BlockSpec((tm, D_MODEL), lambda i: (i, 0))
    gl = lambda n: pl.BlockSpec((tm, D_MODEL), lambda i: (i, SLAB_GL + n))
    full = lambda a: pl.BlockSpec(a.shape, lambda i: (0,) * a.ndim)
    slab = lambda s_: pl.BlockSpec((tm, MIX_W), lambda i: (i, s_))
    in_specs = ([slab(SLAB_UB), slab(SLAB_VB), slab(SLAB_ZB)] + [full(p) for p in sgu_params]
                + [act, act, gl(0), gl(1), gl(2), act, full(bm), full(wb), full(wo)])
    args = [hm, hm, hm, *sgu_params, oa, oc, hm, hm, hm, x2, bm, wb, wo]
    if final:
        in_specs.append(full(final_g))
        args.append(final_g)
    return pl.pallas_call(
        functools.partial(_merge_kernel, final),
        grid=(m // tm,),
        in_specs=in_specs,
        out_specs=act,
        out_shape=jax.ShapeDtypeStruct((m, D_MODEL), F32),
        compiler_params=pltpu.CompilerParams(
            dimension_semantics=("parallel",), vmem_limit_bytes=VMEM_LIMIT),
        name="merge",
    )(*args)


def kernel(x, norm_g, w_in, b_merge, mu_shift, w0, w_w2, a0, a_w2, v0, v_w1, v_w2, k_k, k_a, r_k,
           lnx_g, lnx_b, sgu_ln_g, sgu_ln_b, sgu_w, sgu_b, lb_logits, hgrn_g, w_branch, w_out,
           final_g):
    batch, seq, _ = x.shape
    depth = w_in.shape[0]
    assert seq % min(HGRN_TBLK, seq) == 0 and seq % HGRN_SUB == 0 and x.shape[2] == D_MODEL
    assert seq % RWKV_TBLK == 0 and MERGE_TM % SGU_BLOCK == 0 and seq % SGU_BLOCK == 0
    n_shift = 3 * MIX_W + 2 * R_LORA

    lb_all = jnp.cumsum(jax.nn.softmax(lb_logits.astype(F32), axis=0), axis=0)
    lb_all = lb_all - lb_all[0]

    x2 = x.reshape(batch * seq, D_MODEL)
    wm_all = jnp.concatenate([w_in[:, :, :3 * MIX_W], w_in[:, :, n_shift:]], axis=2).astype(BF16)
    vfirst = None
    for l in range(depth):
        ext = [w_in[l, :, 3 * MIX_W:n_shift]]
        if l > 0:
            ext.append(v_w1[l - 1])
        we = jnp.concatenate(ext, axis=1)
        we = jnp.pad(we, ((0, 0), (0, N_EXT - we.shape[1]))).astype(BF16)

        zrow = jnp.zeros((MIX_W,), F32)
        mu = mu_shift[l]
        rows = [mu[0:MIX_W], mu[MIX_W:2 * MIX_W], mu[2 * MIX_W:3 * MIX_W], w0[l], a0[l], k_k[l],
                k_a[l], r_k[l], lnx_g[l], lnx_b[l],
                v0[l - 1] if l > 0 else zrow,
                jnp.pad(mu[3 * MIX_W:n_shift], (0, MIX_W - 2 * R_LORA))]
        assert len(rows) == PV_ROWS
        pvec = jnp.stack(rows + [zrow] * (2 * SUBLANES - PV_ROWS), axis=0)
        zl = jnp.zeros((R_LORA, MIX_W), F32)
        w2 = jnp.concatenate(
            [jnp.concatenate([w_w2[l], zl], axis=0), jnp.concatenate([zl, a_w2[l]], axis=0)],
            axis=1).astype(BF16)

        hm, he = _proj_in(x2, norm_g[l][None, :], wm_all, l, we)

        if l == 0:
            oa, vfirst = _rwkv(hm, he, None, pvec, w2, None, batch, seq)
        else:
            vw2 = jnp.pad(v_w2[l - 1], ((0, LANES - R_V), (0, 0))).astype(BF16)
            oa = _rwkv(hm, he, vfirst, pvec, w2, vw2, batch, seq)

        bias = jnp.repeat(sgu_b[l].T, LANES, axis=1)
        sgu_params = (sgu_ln_g[l][None, :], sgu_ln_b[l][None, :], sgu_w[l], bias)
        oc = _hgrn(hm, lb_all[l][None, :], hgrn_g[l][None, :], batch, seq)

        bm = jnp.pad(b_merge[l], ((0, SUBLANES - N_BRANCH), (0, 0)))
        x2 = _merge(oa, oc, hm, x2, sgu_params, bm, w_branch[l].astype(BF16), w_out[l].astype(BF16),
                    final_g[None, :] if l == depth - 1 else None)
    return x2.reshape(batch, seq, D_MODEL)
```

```python
import functools
import math

import jax
import jax.numpy as jnp
from jax import lax
from jax.experimental import pallas as pl
from jax.experimental.pallas import tpu as pltpu

F32 = jnp.float32
BF16 = jnp.bfloat16

D_MODEL = 1024
MIX_W = 1024
N_BRANCH = 3
CHUNK = 64
CHUNK_LOG2 = 6
RWKV_TBLK = 512
RWKV_HEAD_W = 64
R_LORA = 64
R_V = 32
SGU_BLOCK = 128
SGU_GROUPS = 8
HGRN_HEAD = 128
HGRN_HEADS = MIX_W // HGRN_HEAD
HGRN_STEP = 16
HGRN_STEP_LOG2 = 4
HGRN_SUB = 512
HGRN_TBLK = 4096
EPS = 1e-6
GN_EPS = 64e-5
LANES = 128
SUBLANES = 8
GROUP_W = 256
N_GROUPS = MIX_W // GROUP_W
N_MAIN = 14 * 1024
N_EXT = 256
VMEM_LIMIT = 48 * 1024 * 1024
PROJ_TM, PROJ_TN = 2048, 1024
MERGE_TM = 512
KK_NORM_FLOOR_SQ = 1e-24
(PV_MU_R, PV_MU_K, PV_MU_V, PV_W0, PV_A0, PV_KK, PV_KA, PV_RK, PV_LNG, PV_LNB, PV_V0,
 PV_MU_WA, PV_ROWS) = range(12 + 1)

SLAB_R, SLAB_K, SLAB_V, SLAB_ZA, SLAB_UB, SLAB_VB, SLAB_ZB = 0, 1, 2, 3, 4, 5, 6
SLAB_QC, SLAB_FC, SLAB_IC, SLAB_ZC, SLAB_GL = 7, 8, 9, 10, 11


def _dot(a, b):
    return jnp.dot(a, b, preferred_element_type=F32)


def _dot_nt(a, b):
    return lax.dot_general(a, b, (((1,), (1,)), ((), ())), preferred_element_type=F32)


def _dot_tn(a, b):
    return lax.dot_general(a, b, (((0,), (0,)), ((), ())), preferred_element_type=F32)


def _split_dot(sel, x, terms):
    acc = None
    rem = x
    for _ in range(terms):
        part = rem.astype(BF16)
        d = _dot(sel, part)
        acc = d if acc is None else acc + d
        rem = rem - part.astype(F32)
    return acc


def _sigmoid(x):
    return jax.nn.sigmoid(x)


def _silu(x):
    return x * jax.nn.sigmoid(x)


def _proj_kernel(x_ref, g_ref, wm_ref, we_ref, hm_ref, he_ref, xn_ref):
    @pl.when(pl.program_id(1) == 0)
    def _():
        x = x_ref[...]
        ms = jnp.mean(x * x, axis=-1, keepdims=True)
        xn = (x * lax.rsqrt(ms + EPS) * g_ref[...]).astype(BF16)
        xn_ref[...] = xn
        he_ref[...] = _dot(xn, we_ref[...])

    hm_ref[...] = _dot(xn_ref[...], wm_ref[...]).astype(hm_ref.dtype)


def _proj_in(x2, g, wm, we):
    m = x2.shape[0]
    tm, tn = min(PROJ_TM, m), PROJ_TN
    return pl.pallas_call(
        _proj_kernel,
        grid=(m // tm, N_MAIN // tn),
        in_specs=[
            pl.BlockSpec((tm, D_MODEL), lambda i, j: (i, 0)),
            pl.BlockSpec((1, D_MODEL), lambda i, j: (0, 0)),
            pl.BlockSpec((D_MODEL, tn), lambda i, j: (0, j)),
            pl.BlockSpec((D_MODEL, N_EXT), lambda i, j: (0, 0)),
        ],
        out_specs=[
            pl.BlockSpec((tm, tn), lambda i, j: (i, j)),
            pl.BlockSpec((tm, N_EXT), lambda i, j: (i, 0)),
        ],
        out_shape=[
            jax.ShapeDtypeStruct((m, N_MAIN), BF16),
            jax.ShapeDtypeStruct((m, N_EXT), F32),
        ],
        scratch_shapes=[pltpu.VMEM((tm, D_MODEL), BF16)],
        compiler_params=pltpu.CompilerParams(
            dimension_semantics=("parallel", "arbitrary"), vmem_limit_bytes=VMEM_LIMIT),
        name="proj_in",
    )(x2, g, wm, we)


def _rwkv_kernel(has_vres, *refs):
    if has_vres:
        (r_ref, k_ref, v_ref, z_ref, wa_ref, vl_ref, vf_ref, pv_ref, w2_ref, vw2_ref,
         o_ref, state_ref, cr_ref, ck_ref, cv_ref, cwa_ref) = refs
    else:
        (r_ref, k_ref, v_ref, z_ref, wa_ref, pv_ref, w2_ref,
         o_ref, vfo_ref, state_ref, cr_ref, ck_ref, cv_ref, cwa_ref) = refs

    @pl.when(pl.program_id(1) == 0)
    def _():
        for ref in (state_ref, cr_ref, ck_ref, cv_ref, cwa_ref):
            ref[...] = jnp.zeros_like(ref)

    row = lax.broadcasted_iota(jnp.int32, (RWKV_TBLK, 1), 0)

    def shift_mix(x_ref, c_ref, mu):
        x = x_ref[...].astype(F32)
        prev = c_ref[SUBLANES - 1:SUBLANES, :]
        xs = jnp.where(row == 0, prev, pltpu.roll(x, 1, 0))
        c_ref[...] = x[RWKV_TBLK - SUBLANES:RWKV_TBLK, :]
        return x + mu * (xs - x)

    pv = pv_ref[...]
    prow = lambda i: pv[i:i + 1]
    mu_r, mu_k, mu_v = prow(PV_MU_R), prow(PV_MU_K), prow(PV_MU_V)
    w0, a0, k_k, k_a, r_k = prow(PV_W0), prow(PV_A0), prow(PV_KK), prow(PV_KA), prow(PV_RK)
    lnx_g, lnx_b, v0 = prow(PV_LNG), prow(PV_LNB), prow(PV_V0)
    mu_wa = prow(PV_MU_WA)[:, 0:LANES]

    r = shift_mix(r_ref, cr_ref, mu_r)
    k = shift_mix(k_ref, ck_ref, mu_k)
    v = shift_mix(v_ref, cv_ref, mu_v)
    wa = shift_mix(wa_ref, cwa_ref, mu_wa)

    lane = lax.broadcasted_iota(jnp.int32, (1, LANES), 1)
    tl = jnp.where(lane < R_LORA, jnp.tanh(wa), wa).astype(BF16)
    pre = _dot(tl, w2_ref[...])
    lw = -math.exp(-0.5) * _sigmoid(w0 + pre[:, :MIX_W])
    a = _sigmoid(a0 + pre[:, MIX_W:])

    if has_vres:
        gate = _sigmoid(v0 + _dot(vl_ref[...].astype(BF16), vw2_ref[...]))
        v = v + (vf_ref[...] - v) * gate
    else:
        vfo_ref[...] = v

    ti = lax.broadcasted_iota(jnp.int32, (RWKV_TBLK, RWKV_TBLK), 0)
    si = lax.broadcasted_iota(jnp.int32, (RWKV_TBLK, RWKV_TBLK), 1)
    same_chunk = jnp.right_shift(ti, CHUNK_LOG2) == jnp.right_shift(si, CHUNK_LOG2)
    tri = jnp.where(same_chunk & (si <= ti), 1.0, 0.0).astype(BF16)
    cum = _split_dot(tri, lw, 2)
    ecum = jnp.exp(cum)
    ecum_prev = jnp.exp(cum - lw)
    einv = jnp.exp(-cum)
    chunks = range(RWKV_TBLK // CHUNK)
    rcs = [slice(c * CHUNK, (c + 1) * CHUNK) for c in chunks]
    cls = [cum[(c + 1) * CHUNK - 1:(c + 1) * CHUNK] for c in chunks]
    g_cs = [jnp.exp(cl) for cl in cls]

    groups = range(N_GROUPS)
    gls = [slice(g * GROUP_W, (g + 1) * GROUP_W) for g in groups]
    units = [(c, g) for c in chunks for g in groups]
    t_i = lax.broadcasted_iota(jnp.int32, (CHUNK, GROUP_W), 0)
    l_i = lax.broadcasted_iota(jnp.int32, (CHUNK, GROUP_W), 1)
    s_i = jnp.bitwise_and(l_i, RWKV_HEAD_W - 1)
    strict = t_i > s_i
    incl = t_i >= s_i
    eye = jnp.where(t_i == s_i, 1.0, 0.0).astype(F32)
    head_of_lane = jnp.right_shift(l_i, CHUNK_LOG2)
    head_masks = [jnp.where(head_of_lane == h, 1.0, 0.0).astype(BF16)
                  for h in range(GROUP_W // RWKV_HEAD_W)]
    bi = lax.broadcasted_iota(jnp.int32, (GROUP_W, GROUP_W), 0)
    bj = lax.broadcasted_iota(jnp.int32, (GROUP_W, GROUP_W), 1)
    same_head = jnp.right_shift(bi, CHUNK_LOG2) == jnp.right_shift(bj, CHUNK_LOG2)
    seg = jnp.where(same_head, 1.0, 0.0).astype(BF16)

    def stack4(x):
        xb = x.astype(BF16)
        return jnp.concatenate([xb * m for m in head_masks], axis=0)

    def segsum(x):
        return _dot(x.astype(BF16), seg)

    kk_raw = k * k_k
    k2 = k * (1.0 + (a - 1.0) * k_a)
    sums2 = [segsum(jnp.concatenate([kk_raw[:, gl] * kk_raw[:, gl],
                                     r[:, gl] * k2[:, gl] * r_k[:, gl]], axis=0)) for gl in gls]
    lhss, rhss, bks, v_sts = {}, {}, {}, {}
    for c, g in units:
        rc, gl = rcs[c], gls[g]
        kk = kk_raw[rc, gl] * lax.rsqrt(jnp.maximum(sums2[g][rc], KK_NORM_FLOOR_SQ))
        kka = kk * a[rc, gl]
        at = -kk * ecum_prev[rc, gl]
        bt = kka * einv[rc, gl]
        kt = k2[rc, gl] * einv[rc, gl]
        rt = r[rc, gl] * ecum[rc, gl]
        lhss[c, g] = jnp.concatenate([at, rt], axis=0).astype(BF16)
        rhss[c, g] = jnp.concatenate([stack4(bt), stack4(kt)], axis=0)
        bks[c, g] = jnp.concatenate([bt, kt], axis=0).astype(BF16)
        v_sts[c, g] = stack4(v[rc, gl])

    gs = {u: _dot_nt(lhss[u], rhss[u]) for u in units}
    n_abs = {u: jnp.where(strict, gs[u][0:CHUNK, 0:GROUP_W], 0.0) for u in units}
    g_ks = {u: jnp.concatenate([jnp.where(strict, gs[u][0:CHUNK, GROUP_W:], 0.0),
                                jnp.where(incl, gs[u][CHUNK:, GROUP_W:], 0.0)],
                               axis=0).astype(BF16) for u in units}
    g_rbs = {u: jnp.where(incl, gs[u][CHUNK:, 0:GROUP_W], 0.0).astype(BF16) for u in units}
    gkvs = {u: _dot(g_ks[u], v_sts[u]) for u in units}

    psts = {u: stack4(n_abs[u]) for u in units}
    pws = {u: _dot(n_abs[u].astype(BF16), psts[u]) for u in units}
    sks = {u: eye + n_abs[u] for u in units}
    for _ in range(4):
        psts = {u: stack4(pws[u]) for u in units}
        prods = {u: _dot(jnp.concatenate([sks[u], pws[u]], axis=0).astype(BF16), psts[u])
                 for u in units}
        sks = {u: sks[u] + prods[u][0:CHUNK] for u in units}
        pws = {u: prods[u][CHUNK:] for u in units}
    psts = {u: stack4(pws[u]) for u in units}
    tinvs = {u: (sks[u] + _dot(sks[u].astype(BF16), psts[u])).astype(BF16) for u in units}

    states = [state_ref[g] for g in groups]
    ys = {}
    for c in chunks:
        ahs = [_dot_nt(lhss[c, g], states[g].astype(BF16)) for g in groups]
        us = [_dot(tinvs[c, g], stack4(ahs[g][0:CHUNK] + gkvs[c, g][0:CHUNK])) for g in groups]
        for g in groups:
            ys[c, g] = (ahs[g][CHUNK:] + gkvs[c, g][CHUNK:]
                        + _dot(g_rbs[c, g], stack4(us[g])))
        uvs = [jnp.concatenate([us[g], v[rcs[c], gls[g]]], axis=0).astype(BF16) for g in groups]
        states = [(states[g] + jnp.where(same_head, _dot_tn(uvs[g], bks[c, g]), 0.0))
                  * g_cs[c][:, gls[g]] for g in groups]
    state_ref[...] = jnp.stack(states, axis=0)

    y_rows = [jnp.concatenate([ys[c, g] for c in chunks], axis=0) for g in groups]
    means = [segsum(y) * (1.0 / RWKV_HEAD_W) for y in y_rows]
    ds = [y_rows[g] - means[g] for g in groups]
    variances = [segsum(d * d) * (1.0 / RWKV_HEAD_W) for d in ds]
    outs = []
    for g, gl in enumerate(gls):
        yn = ds[g] * lax.rsqrt(variances[g] + GN_EPS) * lnx_g[:, gl] + lnx_b[:, gl]
        bonus = sums2[g][RWKV_TBLK:] * v[:, gl]
        outs.append((yn + bonus) * _silu(z_ref[:, gl].astype(F32)))
    o_ref[...] = jnp.concatenate(outs, axis=1).astype(o_ref.dtype)


def _rwkv(hm, he, vfirst, pvec, w2, vw2, batch, seq):
    nt = seq // RWKV_TBLK
    has_vres = vfirst is not None
    tok = lambda b, t: b * nt + t
    slab = lambda s: pl.BlockSpec((RWKV_TBLK, MIX_W), lambda b, t: (tok(b, t), s))
    ext = lambda s: pl.BlockSpec((RWKV_TBLK, LANES), lambda b, t: (tok(b, t), s))
    full = lambda a: pl.BlockSpec(a.shape, lambda b, t: (0,) * a.ndim)
    act = pl.BlockSpec((RWKV_TBLK, MIX_W), lambda b, t: (tok(b, t), 0))
    in_specs = [slab(SLAB_R), slab(SLAB_K), slab(SLAB_V), slab(SLAB_ZA), ext(0)]
    args = [hm, hm, hm, hm, he]
    if has_vres:
        in_specs += [ext(1), act, full(pvec), full(w2), full(vw2)]
        args += [he, vfirst, pvec, w2, vw2]
        out_specs = act
        out_shape = jax.ShapeDtypeStruct((batch * seq, MIX_W), BF16)
    else:
        in_specs += [full(pvec), full(w2)]
        args += [pvec, w2]
        out_specs = [act, act]
        out_shape = [jax.ShapeDtypeStruct((batch * seq, MIX_W), BF16),
                     jax.ShapeDtypeStruct((batch * seq, MIX_W), F32)]
    return pl.pallas_call(
        functools.partial(_rwkv_kernel, has_vres),
        grid=(batch, nt),
        in_specs=in_specs,
        out_specs=out_specs,
        out_shape=out_shape,
        scratch_shapes=[
            pltpu.VMEM((N_GROUPS, GROUP_W, GROUP_W), F32),
            pltpu.VMEM((SUBLANES, MIX_W), F32),
            pltpu.VMEM((SUBLANES, MIX_W), F32),
            pltpu.VMEM((SUBLANES, MIX_W), F32),
            pltpu.VMEM((SUBLANES, LANES), F32),
        ],
        compiler_params=pltpu.CompilerParams(
            dimension_semantics=("parallel", "arbitrary"), vmem_limit_bytes=VMEM_LIMIT),
        name="rwkv",
    )(*args)


def _sgu_rows(n_rows, u_ref, v_ref, z_ref, lng_ref, lnb_ref, ws_ref, bias_ref):
    ti = lax.broadcasted_iota(jnp.int32, (SGU_BLOCK, SGU_BLOCK), 0)
    si = lax.broadcasted_iota(jnp.int32, (SGU_BLOCK, SGU_BLOCK), 1)
    causal = (ti >= CHUNK) | (si < CHUNK)
    wss = [jnp.where(causal, ws_ref[g], 0.0).astype(BF16) for g in range(SGU_GROUPS)]
    blocks = []
    for b in range(n_rows // SGU_BLOCK):
        rows = slice(b * SGU_BLOCK, (b + 1) * SGU_BLOCK)
        v = v_ref[rows, :].astype(F32)
        mean = jnp.mean(v, axis=-1, keepdims=True)
        d = v - mean
        var = jnp.mean(d * d, axis=-1, keepdims=True)
        vn = (d * lax.rsqrt(var + EPS) * lng_ref[...] + lnb_ref[...]).astype(BF16)
        cols = []
        for g in range(SGU_GROUPS):
            sl = slice(g * LANES, (g + 1) * LANES)
            mixed = _dot(wss[g], vn[:, sl]) + bias_ref[:, sl]
            cols.append(u_ref[rows, sl].astype(F32) * mixed * _silu(z_ref[rows, sl].astype(F32)))
        blocks.append(jnp.concatenate(cols, axis=1))
    return jnp.concatenate(blocks, axis=0)


def _hgrn_kernel(n_sub, q_ref, f_ref, i_ref, z_ref, lb_ref, g_ref, o_ref, state_ref, kx_ref, fg_ref,
                 ic_ref):
    @pl.when(pl.program_id(2) == 0)
    def _():
        state_ref[...] = jnp.zeros_like(state_ref)

    for sub in range(n_sub):
        _hgrn_block(slice(sub * HGRN_SUB, (sub + 1) * HGRN_SUB), q_ref, f_ref, i_ref, z_ref,
                    lb_ref, g_ref, o_ref, state_ref, kx_ref, fg_ref, ic_ref)


def _hgrn_block(rows, q_ref, f_ref, i_ref, z_ref, lb_ref, g_ref, o_ref, state_ref, kx_ref, fg_ref,
                ic_ref):
    lb = lb_ref[...]
    gi = lax.broadcasted_iota(jnp.int32, (HGRN_SUB, HGRN_SUB), 0)
    gj = lax.broadcasted_iota(jnp.int32, (HGRN_SUB, HGRN_SUB), 1)
    same_step = jnp.right_shift(gi, HGRN_STEP_LOG2) == jnp.right_shift(gj, HGRN_STEP_LOG2)
    tri = jnp.where(same_step & (gj <= gi), 1.0, 0.0).astype(BF16)
    ones = jnp.ones((HGRN_HEAD, HGRN_HEAD), BF16)
    half = HGRN_STEP // 2
    trow = lax.broadcasted_iota(jnp.int32, (half, 1), 0)

    q = _silu(q_ref[rows, :].astype(F32))
    fg = lb + (1.0 - lb) * _sigmoid(f_ref[rows, :].astype(F32))
    kx = 1.0 - fg
    ic = i_ref[rows, :].astype(F32)
    kx_ref[...] = kx
    fg_ref[...] = fg
    ic_ref[...] = ic

    def row_bcast(ref, r):
        return jnp.broadcast_to(ref[r:r + 1, :], (half, HGRN_HEAD))

    bcs = _split_dot(tri, jnp.log(fg), 2)
    steps = range(HGRN_SUB // HGRN_STEP)
    rss = [slice(j * HGRN_STEP, (j + 1) * HGRN_STEP) for j in steps]
    bls = [bcs[rs][HGRN_STEP - 1:HGRN_STEP] for rs in rss]
    atts_lo, atts_hi = [], []
    for j, rs in enumerate(rss):
        qs, base = q[rs], j * HGRN_STEP
        q_lo, q_hi = qs[0:half], qs[half:HGRN_STEP]
        p_lo, p_hi = [None] * half, [None] * HGRN_STEP
        lo = hi = None
        for s in range(HGRN_STEP - 1, -1, -1):
            nxt = row_bcast(fg_ref, base + s + 1) if s + 1 < HGRN_STEP else None
            k_s = row_bcast(kx_ref, base + s)
            if s >= half:
                prev = 0.0 if hi is None else hi * nxt
                hi = jnp.where(trow == s - half, q_hi, prev)
            else:
                hi = hi * nxt
                prev = 0.0 if lo is None else lo * nxt
                lo = jnp.where(trow == s, q_lo, prev)
                p_lo[s] = lo * k_s
            p_hi[s] = hi * k_s
        stack = jnp.concatenate(p_lo + p_hi, axis=0).astype(BF16)
        att = _dot(stack, ones)
        atts_lo.append(att[0:half * half])
        atts_hi.append(att[half * half:])
    incs = [_dot_tn(ic[rs].astype(BF16), (kx[rs] * jnp.exp(bl - bcs[rs])).astype(BF16))
            for rs, bl in zip(rss, bls)]
    qb = (q * jnp.exp(bcs)).astype(BF16)
    st = state_ref[...]
    o_inters = []
    for j in steps:
        o_inters.append(_dot_nt(qb[rss[j]], st.astype(BF16)))
        st = st * jnp.exp(bls[j]) + incs[j]
    state_ref[...] = st
    outs = []
    for j, rs in enumerate(rss):
        a_lo, a_hi, base = atts_lo[j], atts_hi[j], j * HGRN_STEP
        o_lo = a_lo[0:half] * row_bcast(ic_ref, base)
        for s in range(1, half):
            o_lo = o_lo + a_lo[s * half:(s + 1) * half] * row_bcast(ic_ref, base + s)
        o_hi = a_hi[0:half] * row_bcast(ic_ref, base)
        for s in range(1, HGRN_STEP):
            o_hi = o_hi + a_hi[s * half:(s + 1) * half] * row_bcast(ic_ref, base + s)
        outs.append(o_inters[j] + jnp.concatenate([o_lo, o_hi], axis=0))
    o = jnp.concatenate(outs, axis=0)
    o = o * lax.rsqrt(jnp.mean(o * o, axis=-1, keepdims=True) + EPS) * g_ref[...]
    o_ref[rows, :] = (o * _silu(z_ref[rows, :].astype(F32))).astype(o_ref.dtype)


def _hgrn(hm, lb, g_norm, batch, seq):
    tblk = min(HGRN_TBLK, seq)
    nt = seq // tblk
    col = lambda s: pl.BlockSpec(
        (tblk, HGRN_HEAD), lambda b, h, t: (b * nt + t, s * HGRN_HEADS + h))
    vec = pl.BlockSpec((1, HGRN_HEAD), lambda b, h, t: (0, h))
    return pl.pallas_call(
        functools.partial(_hgrn_kernel, tblk // HGRN_SUB),
        grid=(batch, HGRN_HEADS, nt),
        in_specs=[col(SLAB_QC), col(SLAB_FC), col(SLAB_IC), col(SLAB_ZC), vec, vec],
        out_specs=pl.BlockSpec((tblk, HGRN_HEAD), lambda b, h, t: (b * nt + t, h)),
        out_shape=jax.ShapeDtypeStruct((batch * seq, MIX_W), BF16),
        scratch_shapes=[pltpu.VMEM((HGRN_HEAD, HGRN_HEAD), F32),
                        pltpu.VMEM((HGRN_SUB, HGRN_HEAD), F32),
                        pltpu.VMEM((HGRN_SUB, HGRN_HEAD), F32),
                        pltpu.VMEM((HGRN_SUB, HGRN_HEAD), F32)],
        compiler_params=pltpu.CompilerParams(
            dimension_semantics=("parallel", "parallel", "arbitrary"),
            vmem_limit_bytes=VMEM_LIMIT),
        name="hgrn",
    )(hm, hm, hm, hm, lb, g_norm)


def _merge_kernel(final, *refs):
    sgu_refs, refs = refs[:7], refs[7:]
    if final:
        oa, oc, g0, g1, g2, x_ref, bm_ref, wb_ref, wo_ref, fg_ref, out_ref = refs
    else:
        oa, oc, g0, g1, g2, x_ref, bm_ref, wb_ref, wo_ref, out_ref = refs
    ob = _sgu_rows(x_ref.shape[0], *sgu_refs).astype(BF16)
    y = None
    for n, (o_val, gl_ref) in enumerate(((oa[...], g0), (ob, g1), (oc[...], g2))):
        proj = _dot(o_val, wb_ref[n])
        term = _sigmoid(gl_ref[...].astype(F32) + bm_ref[n:n + 1, :]) * proj
        y = term if y is None else y + term
    xo = x_ref[...] + _dot(y.astype(BF16), wo_ref[...])
    if final:
        ms = jnp.mean(xo * xo, axis=-1, keepdims=True)
        xo = xo * lax.rsqrt(ms + EPS) * fg_ref[...]
    out_ref[...] = xo


def _merge(oa, oc, hm, x2, sgu_params, bm, wb, wo, final_g):
    m = x2.shape[0]
    tm = min(MERGE_TM, m)
    final = final_g is not None
    act = pl.BlockSpec((tm, D_MODEL), lambda i: (i, 0))
    gl = lambda n: pl.BlockSpec((tm, D_MODEL), lambda i: (i, SLAB_GL + n))
    full = lambda a: pl.BlockSpec(a.shape, lambda i: (0,) * a.ndim)
    slab = lambda s_: pl.BlockSpec((tm, MIX_W), lambda i: (i, s_))
    in_specs = ([slab(SLAB_UB), slab(SLAB_VB), slab(SLAB_ZB)] + [full(p) for p in sgu_params]
                + [act, act, gl(0), gl(1), gl(2), act, full(bm), full(wb), full(wo)])
    args = [hm, hm, hm, *sgu_params, oa, oc, hm, hm, hm, x2, bm, wb, wo]
    if final:
        in_specs.append(full(final_g))
        args.append(final_g)
    return pl.pallas_call(
        functools.partial(_merge_kernel, final),
        grid=(m // tm,),
        in_specs=in_specs,
        out_specs=act,
        out_shape=jax.ShapeDtypeStruct((m, D_MODEL), F32),
        compiler_params=pltpu.CompilerParams(
            dimension_semantics=("parallel",), vmem_limit_bytes=VMEM_LIMIT),
        name="merge",
    )(*args)


def kernel(x, norm_g, w_in, b_merge, mu_shift, w0, w_w2, a0, a_w2, v0, v_w1, v_w2, k_k, k_a, r_k,
           lnx_g, lnx_b, sgu_ln_g, sgu_ln_b, sgu_w, sgu_b, lb_logits, hgrn_g, w_branch, w_out,
           final_g):
    batch, seq, _ = x.shape
    depth = w_in.shape[0]
    assert seq % min(HGRN_TBLK, seq) == 0 and seq % HGRN_SUB == 0 and x.shape[2] == D_MODEL
    assert seq % RWKV_TBLK == 0 and MERGE_TM % SGU_BLOCK == 0 and seq % SGU_BLOCK == 0
    n_shift = 3 * MIX_W + 2 * R_LORA

    lb_all = jnp.cumsum(jax.nn.softmax(lb_logits.astype(F32), axis=0), axis=0)
    lb_all = lb_all - lb_all[0]

    x2 = x.reshape(batch * seq, D_MODEL)
    vfirst = None
    for l in range(depth):
        w = w_in[l]
        wm = jnp.concatenate([w[:, :3 * MIX_W], w[:, n_shift:]], axis=1).astype(BF16)
        ext = [w[:, 3 * MIX_W:n_shift]]
        if l > 0:
            ext.append(v_w1[l - 1])
        we = jnp.concatenate(ext, axis=1)
        we = jnp.pad(we, ((0, 0), (0, N_EXT - we.shape[1]))).astype(BF16)

        zrow = jnp.zeros((MIX_W,), F32)
        mu = mu_shift[l]
        rows = [mu[0:MIX_W], mu[MIX_W:2 * MIX_W], mu[2 * MIX_W:3 * MIX_W], w0[l], a0[l], k_k[l],
                k_a[l], r_k[l], lnx_g[l], lnx_b[l],
                v0[l - 1] if l > 0 else zrow,
                jnp.pad(mu[3 * MIX_W:n_shift], (0, MIX_W - 2 * R_LORA))]
        assert len(rows) == PV_ROWS
        pvec = jnp.stack(rows + [zrow] * (2 * SUBLANES - PV_ROWS), axis=0)
        zl = jnp.zeros((R_LORA, MIX_W), F32)
        w2 = jnp.concatenate(
            [jnp.concatenate([w_w2[l], zl], axis=0), jnp.concatenate([zl, a_w2[l]], axis=0)],
            axis=1).astype(BF16)

        hm, he = _proj_in(x2, norm_g[l][None, :], wm, we)

        if l == 0:
            oa, vfirst = _rwkv(hm, he, None, pvec, w2, None, batch, seq)
        else:
            vw2 = jnp.pad(v_w2[l - 1], ((0, LANES - R_V), (0, 0))).astype(BF16)
            oa = _rwkv(hm, he, vfirst, pvec, w2, vw2, batch, seq)

        bias = jnp.repeat(sgu_b[l].T, LANES, axis=1)
        sgu_params = (sgu_ln_g[l][None, :], sgu_ln_b[l][None, :], sgu_w[l], bias)
        oc = _hgrn(hm, lb_all[l][None, :], hgrn_g[l][None, :], batch, seq)

        bm = jnp.pad(b_merge[l], ((0, SUBLANES - N_BRANCH), (0, 0)))
        x2 = _merge(oa, oc, hm, x2, sgu_params, bm, w_branch[l].astype(BF16), w_out[l].astype(BF16),
                    final_g[None, :] if l == depth - 1 else None)
    return x2.reshape(batch, seq, D_MODEL)
```
